```python
import math
import jax
import jax.numpy as jnp
from jax import lax
import numpy as np

D_MODEL = 4096
BATCH = 2
SEQ = 8192
DEPTH = 2

GRID_W = 64
CTX_LEN = 256
N_MOD = 6
N_BRANCH = 3
EPS = 1e-6
CHUNK = 64

CONV_DIM = 1024
CONV_WIDTH = 31

GLA_HEADS = 8
GLA_DK = 128
GLA_DV = 128
GLA_RANK = 16
GLA_TAU = 16.0
GLA_QK = GLA_HEADS * GLA_DK
GLA_V = GLA_HEADS * GLA_DV

SSD_HEADS = 16
SSD_HEADDIM = 64
SSD_GROUPS = 2
SSD_STATE = 128
SSD_CONV = 5
SSD_INNER = SSD_HEADS * SSD_HEADDIM
SSD_XBC = SSD_INNER + 2 * SSD_GROUPS * SSD_STATE

N_EXPERTS = 32
TOP_K = 4
EXPERT_FF = 512
SWIGLU_LIMIT = 7.0
SWIGLU_ALPHA = 1.702

PART_NAMES = ('glu', 'q', 'k', 'v', 'r', 'a_f', 'a_b', 'z', 'xbc', 'dt_f', 'dt_b', 'gate')
IN_SPLITS = (2 * CONV_DIM, GLA_QK, GLA_QK, GLA_V, GLA_V, GLA_RANK, GLA_RANK,
             SSD_INNER, SSD_XBC, SSD_HEADS, SSD_HEADS, N_BRANCH * D_MODEL)
IN_DIM = sum(IN_SPLITS)

kernel_name = 'hybrid_conv_gla_ssd_moe_diffusion_block'


def rmsnorm(x, g):
    xf = x.astype(jnp.float32)
    y = xf * lax.rsqrt(jnp.mean(xf * xf, axis=-1, keepdims=True) + EPS)
    return (y * g.astype(jnp.float32)).astype(x.dtype)


def layernorm(x, g, b):
    xf = x.astype(jnp.float32)
    mu = jnp.mean(xf, axis=-1, keepdims=True)
    var = jnp.mean(jnp.square(xf - mu), axis=-1, keepdims=True)
    y = (xf - mu) * lax.rsqrt(var + EPS) * g.astype(jnp.float32) + b.astype(jnp.float32)
    return y.astype(x.dtype)


def project(h, w):
    points = [int(s) for s in np.cumsum(IN_SPLITS)[:-1]]
    return dict(zip(PART_NAMES, jnp.split(h @ w, points, axis=-1)))


def dwconv(u, w, b):
    k = w.shape[0]
    y = lax.conv_general_dilated(u, w[:, None, :].astype(u.dtype), window_strides=(1,),
                                 padding=[(k // 2, k // 2)],
                                 dimension_numbers=('NWC', 'WIO', 'NWC'),
                                 feature_group_count=u.shape[-1])
    return y + b.astype(u.dtype)


def to_heads(t, n):
    bsz, T, _ = t.shape
    return t.reshape(bsz, T, n, -1).transpose(0, 2, 1, 3)


def conformer_conv(p_glu, w, b, ln_g, ln_b, n_seg, seg_len):
    a, g = jnp.split(p_glu, 2, axis=-1)
    u = a * jax.nn.sigmoid(g)
    bsz, T, ch = u.shape
    y = dwconv(u.reshape(bsz * n_seg, seg_len, ch), w, b).reshape(bsz, T, ch)
    return jax.nn.silu(layernorm(y, ln_g, ln_b))


def gla_prep(pp, lp):
    k = to_heads(pp['k'], GLA_HEADS)
    v = to_heads(pp['v'], GLA_HEADS)

    def log_gate(a, w, b):
        z = (a @ w + b).astype(jnp.float32)
        return to_heads(jax.nn.log_sigmoid(z) / GLA_TAU, GLA_HEADS)

    return (k, v, log_gate(pp['a_f'], lp['gla_wa_f'], lp['gla_ba_f']),
            log_gate(pp['a_b'], lp['gla_wa_b'], lp['gla_ba_b']))


def gla_final_state(k, v, lg):
    k = k.astype(jnp.float32)
    v = v.astype(jnp.float32)
    g = jnp.cumsum(lg, axis=2)
    return jnp.einsum('bhtd,bhtv->bhdv', k * jnp.exp(g[:, :, -1:] - g), v)


def gla_ctx_states(k, v, lg_f, lg_b):
    fl = lambda t: jnp.flip(t, axis=2)
    return (gla_final_state(k, v, lg_f), gla_final_state(fl(k), fl(v), fl(lg_b)))


def gla_chunked(q, k, v, lg, s0):
    q, k, v, lg = (t.astype(jnp.float32) for t in (q, k, v, lg))
    bsz, h, T, dk = q.shape
    n = T // CHUNK
    q = q.reshape(bsz, h, n, CHUNK, dk)
    k = k.reshape(bsz, h, n, CHUNK, dk)
    lg = lg.reshape(bsz, h, n, CHUNK, dk)
    v = v.reshape(bsz, h, n, CHUNK, -1)
    g = jnp.cumsum(lg, axis=3)
    g_ref = g[:, :, :, CHUNK // 2:CHUNK // 2 + 1]
    causal = jnp.tril(jnp.ones((CHUNK, CHUNK), dtype=bool))
    att = jnp.einsum('bhnid,bhnjd->bhnij', q * jnp.exp(g - g_ref), k * jnp.exp(g_ref - g))
    att = jnp.where(causal, att, 0.0)
    y = jnp.einsum('bhnij,bhnjv->bhniv', att, v)
    g_last = g[:, :, :, -1:]
    d_state = jnp.einsum('bhnjd,bhnjv->bhndv', k * jnp.exp(g_last - g), v)

    def step(s, inp):
        dec, ds = inp
        return dec[..., None] * s + ds, s

    _, s_prev = lax.scan(step, s0, (jnp.moveaxis(jnp.exp(g_last[:, :, :, 0]), 2, 0),
                                    jnp.moveaxis(d_state, 2, 0)))
    s_prev = jnp.moveaxis(s_prev, 0, 2)
    y = y + jnp.einsum('bhnid,bhndv->bhniv', q * jnp.exp(g), s_prev)
    return y.reshape(bsz, h, T, -1)


def ssd_prep(pp, lp):
    xbc = jax.nn.silu(dwconv(pp['xbc'], lp['ssd_conv_w'], lp['ssd_conv_b']))
    xs, bm, cm = jnp.split(xbc, [SSD_INNER, SSD_INNER + SSD_GROUPS * SSD_STATE], axis=-1)
    bsz, T, _ = xs.shape
    e = SSD_HEADS // SSD_GROUPS
    xs = xs.reshape(bsz, T, SSD_GROUPS, e, SSD_HEADDIM).astype(jnp.float32)
    bm = bm.reshape(bsz, T, SSD_GROUPS, SSD_STATE).astype(jnp.float32)
    cm = cm.reshape(bsz, T, SSD_GROUPS, SSD_STATE).astype(jnp.float32)

    def disc(dt_raw, dt_bias, a_log):
        dt = jax.nn.softplus(dt_raw.astype(jnp.float32) + dt_bias.astype(jnp.float32))
        dt = dt.reshape(bsz, T, SSD_GROUPS, e)
        a = -jnp.exp(a_log.astype(jnp.float32)).reshape(SSD_GROUPS, e)
        return xs * dt[..., None], a * dt

    xf, af = disc(pp['dt_f'], lp['ssd_dt_bias_f'], lp['ssd_a_log_f'])
    xb, ab = disc(pp['dt_b'], lp['ssd_dt_bias_b'], lp['ssd_a_log_b'])
    return (xs, bm, cm, xf, af, xb, ab)


def ssd_final_state(xd, ad, bm):
    acs = jnp.cumsum(ad, axis=1)
    return jnp.einsum('btgk,btge,btgep->bgepk', bm, jnp.exp(acs[:, -1:] - acs), xd)


def ssd_ctx_states(ssd_in):
    _, bm, _, xf, af, xb, ab = ssd_in
    fl = lambda t: jnp.flip(t, axis=1)
    return (ssd_final_state(xf, af, bm), ssd_final_state(fl(xb), fl(ab), fl(bm)))


def ssd_chunked(xd, ad, bm, cm, h0):
    bsz, T, g, e, p = xd.shape
    n = T // CHUNK
    xd = xd.reshape(bsz, n, CHUNK, g, e, p)
    ad = ad.reshape(bsz, n, CHUNK, g, e)
    bm = bm.reshape(bsz, n, CHUNK, g, -1)
    cm = cm.reshape(bsz, n, CHUNK, g, -1)
    acs = jnp.cumsum(ad, axis=2)
    causal = jnp.tril(jnp.ones((CHUNK, CHUNK), dtype=bool))[None, None, :, :, None, None]
    seg = acs[:, :, :, None] - acs[:, :, None, :]
    lmat = jnp.exp(jnp.where(causal, seg, -jnp.inf))
    cb = jnp.einsum('bnlgk,bnsgk->bnlsg', cm, bm)
    y_diag = jnp.einsum('bnlsg,bnlsge,bnsgep->bnlgep', cb, lmat, xd)
    states = jnp.einsum('bnlgk,bnlge,bnlgep->bngepk', bm, jnp.exp(acs[:, :, -1:] - acs), xd)

    def step(h, inp):
        dec, st = inp
        return dec[..., None, None] * h + st, h

    _, h_prev = lax.scan(step, h0, (jnp.moveaxis(jnp.exp(acs[:, :, -1]), 1, 0),
                                    jnp.moveaxis(states, 1, 0)))
    h_prev = jnp.moveaxis(h_prev, 0, 1)
    y_off = jnp.einsum('bnlgk,bngepk,bnlge->bnlgep', cm, h_prev, jnp.exp(acs))
    return (y_diag + y_off).reshape(bsz, T, g, e, p)


def mix(pp, gla_in, ssd_in, n_seg, seg_len, gla_s, ssd_s, lp):
    f32 = jnp.float32
    dtype = pp['glu'].dtype
    bsz, T, _ = pp['glu'].shape
    y_conv = conformer_conv(pp['glu'], lp['conv_w'], lp['conv_b'], lp['conv_ln_g'],
                            lp['conv_ln_b'], n_seg, seg_len)
    k, v, lg_f, lg_b = gla_in
    q = to_heads(pp['q'], GLA_HEADS) * (GLA_DK ** -0.5)
    fl2 = lambda t: jnp.flip(t, axis=2)
    o = (gla_chunked(q, k, v, lg_f, gla_s[0])
         + fl2(gla_chunked(fl2(q), fl2(k), fl2(v), fl2(lg_b), gla_s[1])))
    o = rmsnorm(o.transpose(0, 2, 1, 3), lp['gla_norm_g'].reshape(GLA_HEADS, GLA_DV))
    r = jax.nn.silu(pp['r'].astype(f32)).reshape(bsz, T, GLA_HEADS, GLA_DV)
    y_gla = (o * r).reshape(bsz, T, GLA_V).astype(dtype)
    xs, bm, cm, xf, af, xb, ab = ssd_in
    fl1 = lambda t: jnp.flip(t, axis=1)
    ys = (ssd_chunked(xf, af, bm, cm, ssd_s[0])
          + fl1(ssd_chunked(fl1(xb), fl1(ab), fl1(bm), fl1(cm), ssd_s[1]))
          + lp['ssd_d'].astype(f32).reshape(SSD_GROUPS, -1, 1) * xs)
    ys = ys.reshape(bsz, T, SSD_INNER) * jax.nn.silu(pp['z'].astype(f32))
    y_ssd = rmsnorm(ys.reshape(bsz, T, SSD_GROUPS, -1),
                    lp['ssd_norm_g'].reshape(SSD_GROUPS, -1)).reshape(bsz, T, SSD_INNER).astype(dtype)
    g_conv, g_gla, g_ssd = jnp.split(jax.nn.sigmoid(pp['gate']), N_BRANCH, axis=-1)
    merged = (g_conv * (y_conv @ lp['w_proj_conv'])
              + g_gla * (y_gla @ lp['w_proj_gla'])
              + g_ssd * (y_ssd @ lp['w_proj_ssd']))
    return merged @ lp['w_out']


def moe(h, lp):
    bsz, T, d = h.shape
    hf = h.reshape(bsz * T, d)
    logits = (hf @ lp['w_router'] + lp['b_router']).astype(jnp.float32)
    top_v, top_i = lax.top_k(logits, TOP_K)
    combine = jnp.einsum('nk,nke->ne', jax.nn.softmax(top_v, axis=-1),
                         jax.nn.one_hot(top_i, N_EXPERTS, dtype=jnp.float32))
    out = jnp.zeros((bsz * T, d), jnp.float32)
    for e in range(N_EXPERTS):
        gu = hf @ lp['w_exp_in'][e] + lp['b_exp_in'][e]
        gate = jnp.minimum(gu[:, :EXPERT_FF], SWIGLU_LIMIT)
        up = jnp.clip(gu[:, EXPERT_FF:], -SWIGLU_LIMIT, SWIGLU_LIMIT)
        act = gate * jax.nn.sigmoid(SWIGLU_ALPHA * gate) * (up + 1.0)
        out = out + combine[:, e:e + 1] * (act @ lp['w_exp_out'][e] + lp['b_exp_out'][e])
    return out.astype(h.dtype).reshape(bsz, T, d)


def setup_inputs(seed: int = 0) -> dict:
    key = jax.random.key(seed)
    ks = iter(jax.random.split(key, 48))
    L, D = DEPTH, D_MODEL
    f32 = jnp.float32

    def nrm(shape, scale):
        return scale * jax.random.normal(next(ks), shape, f32)

    def gain(shape):
        return 1.0 + nrm(shape, 0.02)

    def dt_bias():
        dt = jnp.exp(jax.random.uniform(next(ks), (L, SSD_HEADS), f32,
                                        math.log(1e-3), math.log(1e-1)))
        return dt + jnp.log(-jnp.expm1(-dt))

    def a_log():
        return jnp.log(jax.random.uniform(next(ks), (L, SSD_HEADS), f32, 1.0, 16.0))

    return {
        'x': nrm((BATCH, SEQ, D), 1.0),
        'c': nrm((BATCH, D), 1.0),
        'ctx': nrm((BATCH, CTX_LEN, D), 1.0),
        'c_ctx': nrm((D,), 1.0),
        'g_mix': gain((L, D)),
        'g_ffn': gain((L, D)),
        'w_mod': nrm((L, D, N_MOD * D), 0.5 * D ** -0.5),
        'b_mod': nrm((L, N_MOD * D), 0.02),
        'w_in': nrm((L, D, IN_DIM), D ** -0.5),
        'conv_w': nrm((L, CONV_WIDTH, CONV_DIM), CONV_WIDTH ** -0.5),
        'conv_b': nrm((L, CONV_DIM), 0.02),
        'conv_ln_g': gain((L, CONV_DIM)),
        'conv_ln_b': nrm((L, CONV_DIM), 0.02),
        'w_proj_conv': nrm((L, CONV_DIM, D), CONV_DIM ** -0.5),
        'gla_wa_f': nrm((L, GLA_RANK, GLA_QK), GLA_RANK ** -0.5),
        'gla_ba_f': nrm((L, GLA_QK), 0.1),
        'gla_wa_b': nrm((L, GLA_RANK, GLA_QK), GLA_RANK ** -0.5),
        'gla_ba_b': nrm((L, GLA_QK), 0.1),
        'gla_norm_g': gain((L, GLA_V)),
        'w_proj_gla': nrm((L, GLA_V, D), GLA_V ** -0.5),
        'ssd_conv_w': nrm((L, SSD_CONV, SSD_XBC), SSD_CONV ** -0.5),
        'ssd_conv_b': nrm((L, SSD_XBC), 0.02),
        'ssd_dt_bias_f': dt_bias(),
        'ssd_dt_bias_b': dt_bias(),
        'ssd_a_log_f': a_log(),
        'ssd_a_log_b': a_log(),
        'ssd_d': 1.0 + nrm((L, SSD_HEADS), 0.1),
        'ssd_norm_g': gain((L, SSD_INNER)),
        'w_proj_ssd': nrm((L, SSD_INNER, D), SSD_INNER ** -0.5),
        'w_out': nrm((L, D, D), D ** -0.5),
        'w_router': nrm((L, D, N_EXPERTS), D ** -0.5),
        'b_router': nrm((L, N_EXPERTS), 0.01),
        'w_exp_in': nrm((L, N_EXPERTS, D, 2 * EXPERT_FF), D ** -0.5),
        'b_exp_in': nrm((L, N_EXPERTS, 2 * EXPERT_FF), 0.02),
        'w_exp_out': nrm((L, N_EXPERTS, EXPERT_FF, D), EXPERT_FF ** -0.5),
        'b_exp_out': nrm((L, N_EXPERTS, D), 0.02),
        'g_final': gain((D,)),
    }


def reference(x, c, ctx, c_ctx, g_mix, g_ffn, w_mod, b_mod, w_in, conv_w, conv_b, conv_ln_g,
              conv_ln_b, w_proj_conv, gla_wa_f, gla_ba_f, gla_wa_b, gla_ba_b, gla_norm_g,
              w_proj_gla, ssd_conv_w, ssd_conv_b, ssd_dt_bias_f, ssd_dt_bias_b, ssd_a_log_f,
              ssd_a_log_b, ssd_d, ssd_norm_g, w_proj_ssd, w_out, w_router, b_router, w_exp_in,
              b_exp_in, w_exp_out, b_exp_out, g_final):
    bsz, T, d = x.shape
    rows = T // GRID_W
    ctx_len = ctx.shape[1]
    c_act = jax.nn.silu(c)
    cc_act = jax.nn.silu(c_ctx)
    xl, xc = x, ctx
    for l in range(DEPTH):
        lp = {
            'conv_w': conv_w[l], 'conv_b': conv_b[l], 'conv_ln_g': conv_ln_g[l],
            'conv_ln_b': conv_ln_b[l], 'w_proj_conv': w_proj_conv[l],
            'gla_wa_f': gla_wa_f[l], 'gla_ba_f': gla_ba_f[l], 'gla_wa_b': gla_wa_b[l],
            'gla_ba_b': gla_ba_b[l], 'gla_norm_g': gla_norm_g[l], 'w_proj_gla': w_proj_gla[l],
            'ssd_conv_w': ssd_conv_w[l], 'ssd_conv_b': ssd_conv_b[l],
            'ssd_dt_bias_f': ssd_dt_bias_f[l], 'ssd_dt_bias_b': ssd_dt_bias_b[l],
            'ssd_a_log_f': ssd_a_log_f[l], 'ssd_a_log_b': ssd_a_log_b[l], 'ssd_d': ssd_d[l],
            'ssd_norm_g': ssd_norm_g[l], 'w_proj_ssd': w_proj_ssd[l], 'w_out': w_out[l],
            'w_router': w_router[l], 'b_router': b_router[l], 'w_exp_in': w_exp_in[l],
            'b_exp_in': b_exp_in[l], 'w_exp_out': w_exp_out[l], 'b_exp_out': b_exp_out[l],
        }
        mod = (c_act @ w_mod[l] + b_mod[l]).reshape(bsz, 1, N_MOD, d)
        modc = (cc_act @ w_mod[l] + b_mod[l]).reshape(N_MOD, d)
        hl = rmsnorm(xl, g_mix[l]) * (1.0 + mod[:, :, 1]) + mod[:, :, 0]
        hc = rmsnorm(xc, g_mix[l]) * (1.0 + modc[1]) + modc[0]
        ppl = project(hl, w_in[l])
        ppc = project(hc, w_in[l])
        gla_c = gla_prep(ppc, lp)
        ssd_c = ssd_prep(ppc, lp)
        gla_s = gla_ctx_states(*gla_c)
        ssd_s = ssd_ctx_states(ssd_c)
        xl = xl + mod[:, :, 2] * mix(ppl, gla_prep(ppl, lp), ssd_prep(ppl, lp), rows, GRID_W,
                                     gla_s, ssd_s, lp)
        xl = xl + mod[:, :, 5] * moe(rmsnorm(xl, g_ffn[l]) * (1.0 + mod[:, :, 4]) + mod[:, :, 3], lp)
        if l < DEPTH - 1:
            zero_gla = (jnp.zeros_like(gla_s[0]), jnp.zeros_like(gla_s[1]))
            zero_ssd = (jnp.zeros_like(ssd_s[0]), jnp.zeros_like(ssd_s[1]))
            xc = xc + modc[2] * mix(ppc, gla_c, ssd_c, 1, ctx_len, zero_gla, zero_ssd, lp)
            xc = xc + modc[5] * moe(rmsnorm(xc, g_ffn[l]) * (1.0 + modc[4]) + modc[3], lp)
    return rmsnorm(xl, g_final)
```

```python
import functools
import math

import jax
import jax.numpy as jnp
from jax import lax
from jax.experimental import pallas as pl
from jax.experimental.pallas import tpu as pltpu

F32 = jnp.float32
BF16 = jnp.bfloat16

GRID_W = 64
N_MOD = 6
N_BRANCH = 3
EPS = 1e-6
CHUNK = 64
GLA_DK = 128
GLA_DV = 128
GLA_RANK = 16
GLA_TAU = 16.0
SSD_HEADDIM = 64
SSD_GROUPS = 2
SSD_STATE = 128
TOP_K = 4
SWIGLU_LIMIT = 7.0
SWIGLU_ALPHA = 1.702

LANES = 128
VMEM_LIMIT = 56 * 1024 * 1024
NEG_BIG = -1e30

SM_AF, SM_AB, SM_DTF, SM_DTB = 0, 16, 32, 48


def _cparams(sem):
    return pltpu.CompilerParams(dimension_semantics=sem, vmem_limit_bytes=VMEM_LIMIT)


def _pick(n, pref, mult=8):
    if n <= pref:
        return n
    t = (pref // mult) * mult
    while t >= mult:
        if n % t == 0:
            return t
        t -= mult
    return n


def _dot(a, b):
    return jnp.dot(a.astype(BF16), b.astype(BF16), preferred_element_type=F32)


def _dot_nt(a, b):
    return lax.dot_general(a.astype(BF16), b.astype(BF16), (((1,), (1,)), ((), ())),
                           preferred_element_type=F32)


def _dot_tn(a, b):
    return lax.dot_general(a.astype(BF16), b.astype(BF16), (((0,), (0,)), ((), ())),
                           preferred_element_type=F32)


def _split2(x):
    hi = x.astype(BF16)
    lo = (x - hi.astype(F32)).astype(BF16)
    return hi, lo


def _split3(x):
    hi = x.astype(BF16)
    r = x - hi.astype(F32)
    mid = r.astype(BF16)
    lo = (r - mid.astype(F32)).astype(BF16)
    return hi, mid, lo


def _sel_dot(sel, x):
    hi, mid, lo = _split3(x)
    dot = lambda p: jnp.dot(sel, p, preferred_element_type=F32)
    return dot(hi) + dot(mid) + dot(lo)


def _dot_sel(x, sel):
    hi, lo = _split2(x)
    return (jnp.dot(hi, sel, preferred_element_type=F32)
            + jnp.dot(lo, sel, preferred_element_type=F32))


def _sigmoid(x):
    return 1.0 / (1.0 + jnp.exp(-x))


def _silu(x):
    return x * _sigmoid(x)


def _softplus(x):
    return jnp.maximum(x, 0.0) + jnp.log(1.0 + jnp.exp(-jnp.abs(x)))


def _log_sigmoid(x):
    return jnp.minimum(x, 0.0) - jnp.log(1.0 + jnp.exp(-jnp.abs(x)))


def _mod_kernel(c_ref, w_ref, b_ref, o_ref):
    a = _silu(c_ref[...])
    o_ref[...] = _dot(a, w_ref[...]) + b_ref[...]


def _mod_call(cin, w_mod, b_mod):
    nl, d, n = w_mod.shape
    tn = _pick(n, 512, LANES)
    return pl.pallas_call(
        _mod_kernel,
        grid=(nl, n // tn),
        in_specs=[pl.BlockSpec((8, d), lambda l, j: (0, 0)),
                  pl.BlockSpec((None, d, tn), lambda l, j: (l, 0, j)),
                  pl.BlockSpec((None, 1, tn), lambda l, j: (l, 0, j))],
        out_specs=pl.BlockSpec((None, 8, tn), lambda l, j: (l, 0, j)),
        out_shape=jax.ShapeDtypeStruct((nl, 8, n), F32),
        compiler_params=_cparams(("arbitrary", "arbitrary")),
        name="mod_vectors",
    )(cin, w_mod, b_mod)


def _norm_mod(x, g, sc, sh):
    y = x * lax.rsqrt(jnp.mean(x * x, axis=-1, keepdims=True) + EPS) * g
    return y * (1.0 + sc) + sh


def _norm_kernel(x_ref, g_ref, sc_ref, sh_ref, o_ref):
    o_ref[...] = _norm_mod(x_ref[...], g_ref[...], sc_ref[...], sh_ref[...]).astype(o_ref.dtype)


def _norm_call(x, g, sc, sh, rows_per_group, out_dtype):
    r, d = x.shape
    tm = _pick(rows_per_group, 256)
    per = rows_per_group // tm
    ng = sc.shape[0]
    gidx = (lambda i: (i // per, 0, 0)) if ng > 1 else (lambda i: (0, 0, 0))
    return pl.pallas_call(
        _norm_kernel,
        grid=(r // tm,),
        in_specs=[pl.BlockSpec((tm, d), lambda i: (i, 0)),
                  pl.BlockSpec((1, d), lambda i: (0, 0)),
                  pl.BlockSpec((None, 1, d), gidx),
                  pl.BlockSpec((None, 1, d), gidx)],
        out_specs=pl.BlockSpec((tm, d), lambda i: (i, 0)),
        out_shape=jax.ShapeDtypeStruct((r, d), out_dtype),
        compiler_params=_cparams(("arbitrary",)),
        name="norm_mod",
    )(x, g, sc, sh)


def _mm_kernel(a_ref, b_ref, o_ref):
    o_ref[...] = jnp.dot(a_ref[...], b_ref[...], preferred_element_type=F32)


def _mm_call(a, b, tm_pref, tn):
    r, k = a.shape
    n = b.shape[1]
    tm = _pick(r, tm_pref)
    assert n % tn == 0
    return pl.pallas_call(
        _mm_kernel,
        grid=(n // tn, r // tm),
        in_specs=[pl.BlockSpec((tm, k), lambda j, i: (i, 0)),
                  pl.BlockSpec((k, tn), lambda j, i: (0, j))],
        out_specs=pl.BlockSpec((tm, tn), lambda j, i: (i, j)),
        out_shape=jax.ShapeDtypeStruct((r, n), F32),
        compiler_params=_cparams(("arbitrary", "arbitrary")),
        name="in_proj",
    )(a, b)


def _conv_kernel(seg_len, glu_ref, w_ref, b_ref, lg_ref, lb_ref, o_ref):
    blk, ch = o_ref.shape
    kw = w_ref.shape[0]
    half = kw // 2
    u = glu_ref[:, :ch] * _sigmoid(glu_ref[:, ch:])
    pos = lax.broadcasted_iota(jnp.int32, (blk, 1), 0) % seg_len
    acc = jnp.zeros((blk, ch), F32)
    for j in range(kw):
        d = j - half
        shifted = u if d == 0 else pltpu.roll(u, (-d) % blk, 0)
        valid = jnp.logical_and(pos + d >= 0, pos + d < seg_len)
        acc = acc + jnp.where(valid, shifted, 0.0) * w_ref[j:j + 1, :]
    y = acc + b_ref[...]
    mu = jnp.mean(y, axis=-1, keepdims=True)
    yc = y - mu
    var = jnp.mean(yc * yc, axis=-1, keepdims=True)
    z = yc * lax.rsqrt(var + EPS) * lg_ref[...] + lb_ref[...]
    o_ref[...] = _silu(z).astype(o_ref.dtype)


def _conv_call(proj, conv_w, conv_b, ln_g, ln_b, seg_len):
    r = proj.shape[0]
    kw, ch = conv_w.shape
    blk = seg_len * max(1, 256 // seg_len)
    assert r % blk == 0
    vec = lambda: pl.BlockSpec((1, ch), lambda i: (0, 0))
    return pl.pallas_call(
        functools.partial(_conv_kernel, seg_len),
        grid=(r // blk,),
        in_specs=[pl.BlockSpec((blk, 2 * ch), lambda i: (i, 0)),
                  pl.BlockSpec((kw, ch), lambda i: (0, 0)),
                  vec(), vec(), vec()],
        out_specs=pl.BlockSpec((blk, ch), lambda i: (i, 0)),
        out_shape=jax.ShapeDtypeStruct((r, ch), BF16),
        compiler_params=_cparams(("arbitrary",)),
        name="conformer_conv",
    )(proj, conv_w, conv_b, ln_g, ln_b)


def _ssd_prep_kernel(nblk, cur_ref, prev_ref, next_ref, w_ref, b_ref, o_ref):
    blk, ch = o_ref.shape
    kw = w_ref.shape[0]
    half = kw // 2
    i = pl.program_id(1)
    prev = jnp.where(i > 0, prev_ref[...], 0.0)
    nxt = jnp.where(i < nblk - 1, next_ref[...], 0.0)
    ext = jnp.concatenate([prev, cur_ref[...], nxt], axis=0)
    n = blk + 16
    acc = jnp.zeros((blk, ch), F32)
    for j in range(kw):
        d = j - half
        shifted = ext if d == 0 else pltpu.roll(ext, (-d) % n, 0)
        acc = acc + shifted[8:8 + blk, :] * w_ref[j:j + 1, :]
    o_ref[...] = _silu(acc + b_ref[...])


def _ssd_prep_call(proj, col_blk, width, w, b, nseq, seq_len):
    r = proj.shape[0]
    kw = w.shape[0]
    blk = _pick(seq_len, 256)
    nblk = seq_len // blk
    b8 = blk // 8
    nb8 = r // 8
    cur = lambda s, i: (s * nblk + i, col_blk)
    prev = lambda s, i: (jnp.maximum((s * nblk + i) * b8 - 1, 0), col_blk)
    nxt = lambda s, i: (jnp.minimum((s * nblk + i + 1) * b8, nb8 - 1), col_blk)
    return pl.pallas_call(
        functools.partial(_ssd_prep_kernel, nblk),
        grid=(nseq, nblk),
        in_specs=[pl.BlockSpec((blk, width), cur),
                  pl.BlockSpec((8, width), prev),
                  pl.BlockSpec((8, width), nxt),
                  pl.BlockSpec((kw, width), lambda s, i: (0, 0)),
                  pl.BlockSpec((1, width), lambda s, i: (0, 0))],
        out_specs=pl.BlockSpec((blk, width), lambda s, i: (s * nblk + i, 0)),
        out_shape=jax.ShapeDtypeStruct((r, width), F32),
        compiler_params=_cparams(("arbitrary", "arbitrary")),
        name="ssd_conv",
    )(proj, proj, proj, w, b)


def _gla_kernel(rev, last, nblk, *refs):
    if last:
        (q_ref, k_ref, v_ref, sm_ref, wa_ref, ba_ref, tri_ref, s0_ref, of_ref, r_ref, g_ref,
         o_ref, sfin_ref, st_ref, sp_ref) = refs
    else:
        (q_ref, k_ref, v_ref, sm_ref, wa_ref, ba_ref, tri_ref, s0_ref,
         o_ref, sfin_ref, st_ref, sp_ref) = refs
    i = pl.program_id(2)

    @pl.when(i == 0)
    def _():
        st_ref[...] = s0_ref[...]

    blk = q_ref.shape[0]
    nch = blk // CHUNK
    z = _dot(sm_ref[...], wa_ref[...]) + ba_ref[...]
    lg = _log_sigmoid(z) * (1.0 / GLA_TAU)
    g = _sel_dot(tri_ref[...], lg)
    g3 = g.reshape(nch, CHUNK, GLA_DK)
    ref_i = CHUNK // 2 - 1 if rev else CHUNK // 2
    last_i = 0 if rev else CHUNK - 1
    bcast = lambda t: jnp.broadcast_to(t, (nch, CHUNK, GLA_DK)).reshape(blk, GLA_DK)
    g_ref_pt = bcast(g3[:, ref_i:ref_i + 1, :])
    g_last3 = g3[:, last_i:last_i + 1, :]
    g_last = bcast(g_last3)

    q = q_ref[...] * (GLA_DK ** -0.5)
    k = k_ref[...]
    v = v_ref[...].astype(BF16)
    qe = q * jnp.exp(g - g_ref_pt)
    ke = k * jnp.exp(g_ref_pt - g)
    att = _dot_nt(qe, ke)
    row = lax.broadcasted_iota(jnp.int32, (blk, blk), 0)
    col = lax.broadcasted_iota(jnp.int32, (blk, blk), 1)
    same = (row // CHUNK) == (col // CHUNK)
    tri = (row <= col) if rev else (row >= col)
    att = jnp.where(jnp.logical_and(same, tri), att, 0.0)
    y = _dot(att, v)

    kd = (k * jnp.exp(g_last - g)).astype(BF16)
    qg = (q * jnp.exp(g)).astype(BF16)
    dec = jnp.exp(g_last3)
    st = st_ref[...]
    order = range(nch - 1, -1, -1) if rev else range(nch)
    for c in order:
        sl = slice(c * CHUNK, (c + 1) * CHUNK)
        sp_ref[c] = st
        st = dec[c] * st + _dot_tn(v[sl, :], kd[sl, :])
    st_ref[...] = st
    y_off = [_dot_nt(qg[c * CHUNK:(c + 1) * CHUNK, :], sp_ref[c]) for c in range(nch)]
    y = y + jnp.concatenate(y_off, axis=0)

    @pl.when(i == nblk - 1)
    def _():
        sfin_ref[...] = st

    if last:
        o = of_ref[...] + y
        o = o * lax.rsqrt(jnp.mean(o * o, axis=-1, keepdims=True) + EPS) * g_ref[...]
        o_ref[...] = (o * _silu(r_ref[...])).astype(o_ref.dtype)
    else:
        o_ref[...] = y


def _gla_call(rev, proj, cols, wa_pad, ba, tri, s0, nseq, seq_len, heads, extra=None):
    r = proj.shape[0]
    blk = tri.shape[0]
    nblk = seq_len // blk
    last = extra is not None
    tok = (lambda s, i: s * nblk + (nblk - 1 - i)) if rev else (lambda s, i: s * nblk + i)
    colspec = lambda off: pl.BlockSpec((blk, LANES), lambda s, h, i: (tok(s, i), off + h))
    hvec = pl.BlockSpec((1, LANES), lambda s, h, i: (0, h))
    state = pl.BlockSpec((None, None, GLA_DV, GLA_DK), lambda s, h, i: (s, h, 0, 0))
    in_specs = [colspec(cols["q"]), colspec(cols["k"]), colspec(cols["v"]),
                pl.BlockSpec((blk, LANES), lambda s, h, i: (tok(s, i), cols["small"])),
                pl.BlockSpec((LANES, LANES), lambda s, h, i: (0, h)),
                hvec,
                pl.BlockSpec((blk, blk), lambda s, h, i: (0, 0)),
                state]
    args = [proj, proj, proj, proj, wa_pad, ba, tri, s0]
    if last:
        o_f, norm_g = extra
        in_specs += [pl.BlockSpec((blk, LANES), lambda s, h, i: (tok(s, i), h)),
                     colspec(cols["r"]), hvec]
        args += [o_f, proj, norm_g]
    out_dtype = BF16 if last else F32
    return pl.pallas_call(
        functools.partial(_gla_kernel, rev, last, nblk),
        grid=(nseq, heads, nblk),
        in_specs=in_specs,
        out_specs=[pl.BlockSpec((blk, LANES), lambda s, h, i: (tok(s, i), h)), state],
        out_shape=[jax.ShapeDtypeStruct((r, heads * GLA_DV), out_dtype),
                   jax.ShapeDtypeStruct((nseq, heads, GLA_DV, GLA_DK), F32)],
        scratch_shapes=[pltpu.VMEM((GLA_DV, GLA_DK), F32),
                        pltpu.VMEM((blk // CHUNK, GLA_DV, GLA_DK), F32)],
        compiler_params=_cparams(("arbitrary", "arbitrary", "arbitrary")),
        name="gla_bwd" if rev else "gla_fwd",
    )(*args)


def _ssd_kernel(rev, last, nblk, lane0, *refs):
    if last:
        (xs_ref, bc_ref, sm_ref, dtb_ref, a_ref, tri_ref, ex_ref, h0_ref, yf_ref, z_ref, d_ref,
         ng_ref, o_ref, hfin_ref, h_ref) = refs
    else:
        (xs_ref, bc_ref, sm_ref, dtb_ref, a_ref, tri_ref, ex_ref, h0_ref,
         o_ref, hfin_ref, h_ref) = refs
    i = pl.program_id(1)

    @pl.when(i == 0)
    def _():
        h_ref[...] = h0_ref[...]

    blk, inner = xs_ref.shape
    gsz = inner // SSD_GROUPS
    epg = gsz // SSD_HEADDIM
    ns = SSD_STATE

    dt = _softplus(sm_ref[...] + dtb_ref[...])
    ad = a_ref[...] * dt
    cum = _sel_dot(tri_ref[...], ad)
    tot = cum[0:1, :] if rev else cum[blk - 1:blk, :]
    cum_t = cum.T
    ex = ex_ref[...]
    xs = xs_ref[...]
    xd = (xs * _dot_sel(dt, ex)).astype(BF16)
    xdw = (xs * _dot_sel(dt * jnp.exp(tot - cum), ex)).astype(BF16)
    ecum = _dot_sel(jnp.exp(cum), ex)

    row = lax.broadcasted_iota(jnp.int32, (blk, blk), 0)
    col = lax.broadcasted_iota(jnp.int32, (blk, blk), 1)
    causal = (row <= col) if rev else (row >= col)
    lane = lax.broadcasted_iota(jnp.int32, (blk, LANES), 1)
    low = lane < SSD_HEADDIM

    y_cols = []
    for g in range(SSD_GROUPS):
        bm = bc_ref[:, g * ns:(g + 1) * ns].astype(BF16)
        cm = bc_ref[:, (SSD_GROUPS + g) * ns:(SSD_GROUPS + g + 1) * ns].astype(BF16)
        cb = _dot_nt(cm, bm)
        hg = h_ref[g]
        y_off = _dot_nt(cm, hg)
        st_new = _dot_tn(xdw[:, g * gsz:(g + 1) * gsz], bm)
        for pr in range(epg // 2):
            c0 = g * gsz + pr * LANES
            xpair = xd[:, c0:c0 + LANES]
            acc = y_off[:, pr * LANES:(pr + 1) * LANES] * ecum[:, c0:c0 + LANES]
            for half in range(2):
                ln = lane0 + g * epg + 2 * pr + half
                seg = cum[:, ln:ln + 1] - cum_t[ln:ln + 1, :]
                w = (cb * jnp.exp(jnp.where(causal, seg, NEG_BIG))).astype(BF16)
                keep = low if half == 0 else jnp.logical_not(low)
                rhs = jnp.where(keep, xpair, jnp.zeros_like(xpair))
                acc = acc + jnp.dot(w, rhs, preferred_element_type=F32)
            y_cols.append(acc)
        for e in range(epg):
            ln = lane0 + g * epg + e
            rs = slice(e * SSD_HEADDIM, (e + 1) * SSD_HEADDIM)
            dec = jnp.broadcast_to(jnp.exp(tot[:, ln:ln + 1]), (SSD_HEADDIM, ns))
            h_ref[g, rs, :] = dec * hg[rs, :] + st_new[rs, :]
    y = jnp.concatenate(y_cols, axis=1)

    @pl.when(i == nblk - 1)
    def _():
        hfin_ref[...] = h_ref[...]

    if last:
        ys = (yf_ref[...] + y + d_ref[...] * xs) * _silu(z_ref[...])
        outs = []
        for g in range(SSD_GROUPS):
            yg = ys[:, g * gsz:(g + 1) * gsz]
            outs.append(yg * lax.rsqrt(jnp.mean(yg * yg, axis=-1, keepdims=True) + EPS))
        o_ref[...] = (jnp.concatenate(outs, axis=1) * ng_ref[...]).astype(o_ref.dtype)
    else:
        o_ref[...] = y


def _ssd_call(rev, xs, bc, proj, cols, dtb, a_row, tri, ex, h0, nseq, seq_len, extra=None):
    r, inner = xs.shape
    blk = tri.shape[0]
    nblk = seq_len // blk
    last = extra is not None
    gsz = inner // SSD_GROUPS
    lane0 = SM_DTB if rev else SM_DTF
    tok = (lambda s, i: s * nblk + (nblk - 1 - i)) if rev else (lambda s, i: s * nblk + i)
    rowblk = lambda w: pl.BlockSpec((blk, w), lambda s, i: (tok(s, i), 0))
    const = lambda shp: pl.BlockSpec(shp, lambda s, i: (0,) * len(shp))
    state = pl.BlockSpec((None, SSD_GROUPS, gsz, SSD_STATE), lambda s, i: (s, 0, 0, 0))
    in_specs = [rowblk(inner), rowblk(bc.shape[1]),
                pl.BlockSpec((blk, LANES), lambda s, i: (tok(s, i), cols["small"])),
                const((1, LANES)), const((1, LANES)), const((blk, blk)), const((LANES, inner)),
                state]
    args = [xs, bc, proj, dtb, a_row, tri, ex, h0]
    if last:
        y_f, dskip, norm_g = extra
        in_specs += [rowblk(inner),
                     pl.BlockSpec((blk, inner), lambda s, i: (tok(s, i), cols["z"])),
                     const((1, inner)), const((1, inner))]
        args += [y_f, proj, dskip, norm_g]
    return pl.pallas_call(
        functools.partial(_ssd_kernel, rev, last, nblk, lane0),
        grid=(nseq, nblk),
        in_specs=in_specs,
        out_specs=[rowblk(inner), state],
        out_shape=[jax.ShapeDtypeStruct((r, inner), BF16 if last else F32),
                   jax.ShapeDtypeStruct((nseq, SSD_GROUPS, gsz, SSD_STATE), F32)],
        scratch_shapes=[pltpu.VMEM((SSD_GROUPS, gsz, SSD_STATE), F32)],
        compiler_params=_cparams(("arbitrary", "arbitrary")),
        name="ssd_bwd" if rev else "ssd_fwd",
    )(*args)


def _merge_kernel(gc_ref, gg_ref, gs_ref, yc_ref, yg_ref, ys_ref, wc_ref, wg_ref, ws_ref, o_ref):
    dot = lambda a, b: jnp.dot(a[...], b[...], preferred_element_type=F32)
    acc = _sigmoid(gc_ref[...]) * dot(yc_ref, wc_ref)
    acc = acc + _sigmoid(gg_ref[...]) * dot(yg_ref, wg_ref)
    acc = acc + _sigmoid(gs_ref[...]) * dot(ys_ref, ws_ref)
    o_ref[...] = acc.astype(o_ref.dtype)


def _merge_call(proj, gate_col, y_conv, y_gla, y_ssd, w_c, w_g, w_s, tn):
    r = proj.shape[0]
    d = w_c.shape[1]
    tm = _pick(r, 512)
    nj = d // tn
    gate = lambda b: pl.BlockSpec((tm, tn), lambda j, i: (i, gate_col + b * nj + j))
    yspec = lambda y: pl.BlockSpec((tm, y.shape[1]), lambda j, i: (i, 0))
    wspec = lambda w: pl.BlockSpec((w.shape[0], tn), lambda j, i: (0, j))
    return pl.pallas_call(
        _merge_kernel,
        grid=(nj, r // tm),
        in_specs=[gate(0), gate(1), gate(2), yspec(y_conv), yspec(y_gla), yspec(y_ssd),
                  wspec(w_c), wspec(w_g), wspec(w_s)],
        out_specs=pl.BlockSpec((tm, tn), lambda j, i: (i, j)),
        out_shape=jax.ShapeDtypeStruct((r, d), BF16),
        compiler_params=_cparams(("arbitrary", "arbitrary")),
        name="gated_merge",
    )(proj, proj, proj, y_conv, y_gla, y_ssd, w_c, w_g, w_s)


def _out_kernel(m_ref, w_ref, x_ref, gt_ref, o_ref):
    o_ref[...] = x_ref[...] + gt_ref[...] * jnp.dot(m_ref[...], w_ref[...],
                                                     preferred_element_type=F32)


def _out_call(merged, w_out, x, gate, rows_per_group):
    r, d = x.shape
    tm = _pick(rows_per_group, 512)
    tn = _pick(d, 1024, LANES)
    per = rows_per_group // tm
    gidx = (lambda j, i: (i // per, 0, j)) if gate.shape[0] > 1 else (lambda j, i: (0, 0, j))
    return pl.pallas_call(
        _out_kernel,
        grid=(d // tn, r // tm),
        in_specs=[pl.BlockSpec((tm, d), lambda j, i: (i, 0)),
                  pl.BlockSpec((d, tn), lambda j, i: (0, j)),
                  pl.BlockSpec((tm, tn), lambda j, i: (i, j)),
                  pl.BlockSpec((None, 1, tn), gidx)],
        out_specs=pl.BlockSpec((tm, tn), lambda j, i: (i, j)),
        out_shape=jax.ShapeDtypeStruct((r, d), F32),
        compiler_params=_cparams(("arbitrary", "arbitrary")),
        name="out_proj",
    )(merged, w_out, x, gate)


def _router_kernel(x_ref, g_ref, sc_ref, sh_ref, wr_ref, br_ref, h_ref, comb_ref):
    h = _norm_mod(x_ref[...], g_ref[...], sc_ref[...], sh_ref[...])
    h_ref[...] = h.astype(BF16)
    hhi, hlo = _split2(h)
    whi, wmid, wlo = wr_ref[0], wr_ref[1], wr_ref[2]
    dot = lambda a, b: jnp.dot(a, b, preferred_element_type=F32)
    logits = (dot(hhi, whi) + dot(hhi, wmid) + dot(hlo, whi) + dot(hhi, wlo) + dot(hlo, wmid)
              + br_ref[...])
    tm, ne = logits.shape
    lane = lax.broadcasted_iota(jnp.int32, (tm, ne), 1).astype(F32)
    work = logits
    sels, vals = [], []
    for _ in range(TOP_K):
        m = jnp.max(work, axis=-1, keepdims=True)
        idx = jnp.min(jnp.where(work == m, lane, float(ne)), axis=-1, keepdims=True)
        sel = lane == idx
        sels.append(sel)
        vals.append(m)
        work = jnp.where(sel, -jnp.inf, work)
    es = [jnp.exp(v - vals[0]) for v in vals]
    inv = 1.0 / sum(es)
    comb = jnp.zeros((tm, ne), F32)
    for sel, e in zip(sels, es):
        comb = comb + jnp.where(sel, e * inv, 0.0)
    comb_ref[...] = comb


def _router_call(x, g, sc, sh, wr3, br, rows_per_group):
    r, d = x.shape
    ne = wr3.shape[2]
    tm = _pick(rows_per_group, 256)
    per = rows_per_group // tm
    gidx = (lambda i: (i // per, 0, 0)) if sc.shape[0] > 1 else (lambda i: (0, 0, 0))
    return pl.pallas_call(
        _router_kernel,
        grid=(r // tm,),
        in_specs=[pl.BlockSpec((tm, d), lambda i: (i, 0)),
                  pl.BlockSpec((1, d), lambda i: (0, 0)),
                  pl.BlockSpec((None, 1, d), gidx),
                  pl.BlockSpec((None, 1, d), gidx),
                  pl.BlockSpec((3, d, ne), lambda i: (0, 0, 0)),
                  pl.BlockSpec((1, ne), lambda i: (0, 0))],
        out_specs=[pl.BlockSpec((tm, d), lambda i: (i, 0)),
                   pl.BlockSpec((tm, ne), lambda i: (i, 0))],
        out_shape=[jax.ShapeDtypeStruct((r, d), BF16), jax.ShapeDtypeStruct((r, ne), F32)],
        compiler_params=_cparams(("arbitrary",)),
        name="router",
    )(x, g, sc, sh, wr3, br)


def _moe_kernel(ne, h_ref, comb_ref, wi_ref, bi_ref, wo_ref, bo_ref, x_ref, gt_ref, o_ref):
    e = pl.program_id(1)

    @pl.when(e == 0)
    def _():
        o_ref[...] = jnp.zeros_like(o_ref)

    ff = wo_ref.shape[0]
    gu = jnp.dot(h_ref[...], wi_ref[...], preferred_element_type=F32) + bi_ref[...]
    gate = jnp.minimum(gu[:, :ff], SWIGLU_LIMIT)
    up = jnp.clip(gu[:, ff:], -SWIGLU_LIMIT, SWIGLU_LIMIT)
    act = gate * _sigmoid(SWIGLU_ALPHA * gate) * (up + 1.0)
    y = jnp.dot(act.astype(BF16), wo_ref[...], preferred_element_type=F32) + bo_ref[...]
    comb = comb_ref[...]
    lane = lax.broadcasted_iota(jnp.int32, comb.shape, 1)
    wcol = jnp.sum(jnp.where(lane == e, comb, 0.0), axis=-1, keepdims=True)
    o_ref[...] += wcol * y

    @pl.when(e == ne - 1)
    def _():
        o_ref[...] = x_ref[...] + gt_ref[...] * o_ref[...]


def _moe_call(h, comb, w_in, b_in, w_out, b_out, x, gate, rows_per_group):
    r, d = x.shape
    ne, _, ff2 = w_in.shape
    ff = ff2 // 2
    tm = _pick(rows_per_group, 256)
    per = rows_per_group // tm
    gidx = (lambda i, e: (i // per, 0, 0)) if gate.shape[0] > 1 else (lambda i, e: (0, 0, 0))
    return pl.pallas_call(
        functools.partial(_moe_kernel, ne),
        grid=(r // tm, ne),
        in_specs=[pl.BlockSpec((tm, d), lambda i, e: (i, 0)),
                  pl.BlockSpec((tm, ne), lambda i, e: (i, 0)),
                  pl.BlockSpec((None, d, ff2), lambda i, e: (e, 0, 0)),
                  pl.BlockSpec((None, 1, ff2), lambda i, e: (e, 0, 0)),
                  pl.BlockSpec((None, ff, d), lambda i, e: (e, 0, 0)),
                  pl.BlockSpec((None, 1, d), lambda i, e: (e, 0, 0)),
                  pl.BlockSpec((tm, d), lambda i, e: (i, 0)),
                  pl.BlockSpec((None, 1, d), gidx)],
        out_specs=pl.BlockSpec((tm, d), lambda i, e: (i, 0)),
        out_shape=jax.ShapeDtypeStruct((r, d), F32),
        compiler_params=_cparams(("arbitrary", "arbitrary")),
        name="moe_experts",
    )(h, comb, w_in, b_in, w_out, b_out, x, gate)


def _tri_blockdiag(blk, chunk, upper):
    i = jnp.arange(blk)
    same = (i[:, None] // chunk) == (i[None, :] // chunk)
    tri = (i[:, None] <= i[None, :]) if upper else (i[:, None] >= i[None, :])
    return jnp.logical_and(same, tri).astype(BF16)


def _layer_params(l, p, dims):
    d, ch, qk, inner, heads_ssd = dims["d"], dims["ch"], dims["qk"], dims["inner"], dims["ssd_heads"]
    w_in = p["w_in"][l]
    xbc = inner + 2 * SSD_GROUPS * SSD_STATE
    sizes = [("glu", 2 * ch), ("q", qk), ("k", qk), ("v", qk), ("r", qk), ("a_f", GLA_RANK),
             ("a_b", GLA_RANK), ("z", inner), ("xbc", xbc), ("dt_f", heads_ssd),
             ("dt_b", heads_ssd), ("gate", N_BRANCH * d)]
    off, o = {}, 0
    for name, s in sizes:
        off[name] = (o, s)
        o += s
    col = lambda name, a=0, b=None: w_in[:, off[name][0] + a: off[name][0] + (off[name][1] if b is None else b)]
    small = jnp.zeros((d, LANES), F32)
    small = small.at[:, SM_AF:SM_AF + GLA_RANK].set(col("a_f"))
    small = small.at[:, SM_AB:SM_AB + GLA_RANK].set(col("a_b"))
    small = small.at[:, SM_DTF:SM_DTF + heads_ssd].set(col("dt_f"))
    small = small.at[:, SM_DTB:SM_DTB + heads_ssd].set(col("dt_b"))
    parts = [col("glu"), col("q"), col("k"), col("v"), col("r"), col("z"),
             col("xbc", 0, inner), col("xbc", inner), col("gate"), small]
    w_r = jnp.concatenate(parts, axis=1)
    n_real = w_r.shape[1]
    tn = dims["tn_in"]
    n_pad = -(-n_real // tn) * tn
    w_r = jnp.pad(w_r, ((0, 0), (0, n_pad - n_real))).astype(BF16)

    def wa_pad(wa, lane0):
        return jnp.zeros((LANES, qk), F32).at[lane0:lane0 + GLA_RANK].set(wa).astype(BF16)

    def lane_row(vec, lane0):
        return jnp.zeros((1, LANES), F32).at[0, lane0:lane0 + heads_ssd].set(vec)

    conv_w = p["ssd_conv_w"][l]
    conv_b = p["ssd_conv_b"][l][None, :]
    wr = p["w_router"][l]
    whi = wr.astype(BF16)
    r1 = wr - whi.astype(F32)
    wmid = r1.astype(BF16)
    wlo = (r1 - wmid.astype(F32)).astype(BF16)
    return dict(
        w_in=w_r,
        conv_w=p["conv_w"][l], conv_b=p["conv_b"][l][None], ln_g=p["conv_ln_g"][l][None],
        ln_b=p["conv_ln_b"][l][None],
        wa_f=wa_pad(p["gla_wa_f"][l], SM_AF), wa_b=wa_pad(p["gla_wa_b"][l], SM_AB),
        ba_f=p["gla_ba_f"][l][None], ba_b=p["gla_ba_b"][l][None],
        gla_g=p["gla_norm_g"][l][None],
        cw_x=conv_w[:, :inner], cw_bc=conv_w[:, inner:], cb_x=conv_b[:, :inner], cb_bc=conv_b[:, inner:],
        dtb=lane_row(p["ssd_dt_bias_f"][l], SM_DTF) + lane_row(p["ssd_dt_bias_b"][l], SM_DTB),
        a_f=lane_row(-jnp.exp(p["ssd_a_log_f"][l]), SM_DTF),
        a_b=lane_row(-jnp.exp(p["ssd_a_log_b"][l]), SM_DTB),
        dskip=jnp.repeat(p["ssd_d"][l], SSD_HEADDIM)[None],
        ssd_g=p["ssd_norm_g"][l][None],
        w_c=p["w_proj_conv"][l].astype(BF16), w_g=p["w_proj_gla"][l].astype(BF16),
        w_s=p["w_proj_ssd"][l].astype(BF16), w_out=p["w_out"][l].astype(BF16),
        wr3=jnp.stack([whi, wmid, wlo]), br=p["b_router"][l][None],
        we_in=p["w_exp_in"][l].astype(BF16), be_in=p["b_exp_in"][l][:, None, :],
        we_out=p["w_exp_out"][l].astype(BF16), be_out=p["b_exp_out"][l][:, None, :],
        g_mix=p["g_mix"][l][None], g_ffn=p["g_ffn"][l][None],
    )


def _expand_matrix(lane0, heads, inner):
    lanes = jnp.arange(LANES)[:, None]
    cols = jnp.arange(inner)[None, :]
    return (lanes == lane0 + cols // SSD_HEADDIM).astype(BF16)


def _stream(x, mod, lp, dims, cols, nseq, seq_len, seg_len, states, full):
    inner, qk = dims["inner"], dims["qk"]
    h = _norm_call(x, lp["g_mix"], mod[1], mod[0], seq_len, BF16)
    proj = _mm_call(h, lp["w_in"], 512, dims["tn_in"])

    blk = _pick(seq_len, 256)
    tri_lo = _tri_blockdiag(blk, CHUNK, False)
    tri_up = _tri_blockdiag(blk, CHUNK, True)
    heads = qk // GLA_DK
    o_f, gs_f = _gla_call(False, proj, cols, lp["wa_f"], lp["ba_f"], tri_lo, states[0], nseq, seq_len, heads)
    y_gla, gs_b = _gla_call(True, proj, cols, lp["wa_b"], lp["ba_b"], tri_up, states[1], nseq, seq_len,
                            heads, extra=(o_f, lp["gla_g"]))

    xs = _ssd_prep_call(proj, cols["xs"] * LANES // inner, inner, lp["cw_x"], lp["cb_x"], nseq, seq_len)
    wbc = 2 * SSD_GROUPS * SSD_STATE
    bc = _ssd_prep_call(proj, cols["bc"] * LANES // wbc, wbc, lp["cw_bc"], lp["cb_bc"], nseq, seq_len)
    full_lo = _tri_blockdiag(blk, blk, False)
    full_up = _tri_blockdiag(blk, blk, True)
    ssd_heads = inner // SSD_HEADDIM
    ex_f = _expand_matrix(SM_DTF, ssd_heads, inner)
    ex_b = _expand_matrix(SM_DTB, ssd_heads, inner)
    scols = dict(small=cols["small"], z=cols["z"] * LANES // inner)
    y_f, hs_f = _ssd_call(False, xs, bc, proj, scols, lp["dtb"], lp["a_f"], full_lo, ex_f, states[2],
                          nseq, seq_len)
    y_ssd, hs_b = _ssd_call(True, xs, bc, proj, scols, lp["dtb"], lp["a_b"], full_up, ex_b, states[3],
                            nseq, seq_len, extra=(y_f, lp["dskip"], lp["ssd_g"]))
    finals = (gs_f, gs_b, hs_f, hs_b)
    if not full:
        return None, finals

    y_conv = _conv_call(proj, lp["conv_w"], lp["conv_b"], lp["ln_g"], lp["ln_b"], seg_len)
    tn = dims["tn_merge"]
    merged = _merge_call(proj, cols["gate"] * LANES // tn, y_conv, y_gla, y_ssd,
                         lp["w_c"], lp["w_g"], lp["w_s"], tn)
    x1 = _out_call(merged, lp["w_out"], x, mod[2], seq_len)
    h2, comb = _router_call(x1, lp["g_ffn"], mod[4], mod[3], lp["wr3"], lp["br"], seq_len)
    x2 = _moe_call(h2, comb, lp["we_in"], lp["be_in"], lp["we_out"], lp["be_out"], x1, mod[5], seq_len)
    return x2, finals


def kernel(x, c, ctx, c_ctx, g_mix, g_ffn, w_mod, b_mod, w_in, conv_w, conv_b, conv_ln_g, conv_ln_b,
           w_proj_conv, gla_wa_f, gla_ba_f, gla_wa_b, gla_ba_b, gla_norm_g, w_proj_gla, ssd_conv_w,
           ssd_conv_b, ssd_dt_bias_f, ssd_dt_bias_b, ssd_a_log_f, ssd_a_log_b, ssd_d, ssd_norm_g,
           w_proj_ssd, w_out, w_router, b_router, w_exp_in, b_exp_in, w_exp_out, b_exp_out, g_final):
    p = dict(g_mix=g_mix, g_ffn=g_ffn, w_in=w_in, conv_w=conv_w, conv_b=conv_b, conv_ln_g=conv_ln_g,
             conv_ln_b=conv_ln_b, w_proj_conv=w_proj_conv, gla_wa_f=gla_wa_f, gla_ba_f=gla_ba_f,
             gla_wa_b=gla_wa_b, gla_ba_b=gla_ba_b, gla_norm_g=gla_norm_g, w_proj_gla=w_proj_gla,
             ssd_conv_w=ssd_conv_w, ssd_conv_b=ssd_conv_b, ssd_dt_bias_f=ssd_dt_bias_f,
             ssd_dt_bias_b=ssd_dt_bias_b, ssd_a_log_f=ssd_a_log_f, ssd_a_log_b=ssd_a_log_b,
             ssd_d=ssd_d, ssd_norm_g=ssd_norm_g, w_proj_ssd=w_proj_ssd, w_out=w_out,
             w_router=w_router, b_router=b_router, w_exp_in=w_exp_in, b_exp_in=b_exp_in,
             w_exp_out=w_exp_out, b_exp_out=b_exp_out)
    bsz, t, d = x.shape
    ctx_len = ctx.shape[1]
    depth = w_mod.shape[0]
    ch = conv_w.shape[2]
    qk = gla_wa_f.shape[2]
    ssd_heads = ssd_d.shape[1]
    inner = ssd_heads * SSD_HEADDIM
    assert ssd_heads <= 16 and bsz + 1 <= 8
    widths = [("glu", 2 * ch), ("q", qk), ("k", qk), ("v", qk), ("r", qk), ("z", inner), ("xs", inner),
              ("bc", 2 * SSD_GROUPS * SSD_STATE), ("gate", N_BRANCH * d), ("small", LANES)]
    cols, o = {}, 0
    for name, wdt in widths:
        assert o % LANES == 0
        cols[name] = o // LANES
        o += wdt
    tn_merge = math.gcd(cols["gate"] * LANES, _pick(d, 1024, LANES))
    dims = dict(d=d, ch=ch, qk=qk, inner=inner, ssd_heads=ssd_heads,
                tn_in=min(1536, o), tn_merge=tn_merge)
    assert (cols["z"] * LANES) % inner == 0 and (cols["xs"] * LANES) % inner == 0
    assert (cols["bc"] * LANES) % (2 * SSD_GROUPS * SSD_STATE) == 0
    assert (cols["gate"] * LANES) % tn_merge == 0 and (cols["glu"] == 0)

    cin = jnp.zeros((8, d), F32).at[:bsz].set(c).at[bsz].set(c_ctx)
    mods = _mod_call(cin, w_mod, b_mod[:, None, :])

    xl = x.reshape(bsz * t, d)
    xc = ctx.reshape(bsz * ctx_len, d)
    heads = qk // GLA_DK
    zero_g = jnp.zeros((bsz, heads, GLA_DV, GLA_DK), F32)
    zero_s = jnp.zeros((bsz, SSD_GROUPS, inner // SSD_GROUPS, SSD_STATE), F32)
    zeros = (zero_g, zero_g, zero_s, zero_s)
    for l in range(depth):
        lp = _layer_params(l, p, dims)
        m = mods[l].reshape(8, N_MOD, d)
        mod_l = jnp.transpose(m[:bsz], (1, 0, 2))[:, :, None, :]
        mod_c = m[bsz][:, None, None, :]
        last = l == depth - 1
        xc_new, ctx_states = _stream(xc, mod_c, lp, dims, cols, bsz, ctx_len, ctx_len, zeros,
                                     full=not last)
        xl, _ = _stream(xl, mod_l, lp, dims, cols, bsz, t, GRID_W, ctx_states, full=True)
        if not last:
            xc = xc_new
    zero_row = jnp.zeros((1, 1, d), F32)
    out = _norm_call(xl, g_final[None], zero_row, zero_row, t, F32)
    return out.reshape(bsz, t, d)
```

```python
import functools
import math

import jax
import jax.numpy as jnp
from jax import lax
from jax.experimental import pallas as pl
from jax.experimental.pallas import tpu as pltpu

F32 = jnp.float32
BF16 = jnp.bfloat16

GRID_W = 64
N_MOD = 6
N_BRANCH = 3
EPS = 1e-6
CHUNK = 64
GLA_DK = 128
GLA_DV = 128
GLA_RANK = 16
GLA_TAU = 16.0
SSD_HEADDIM = 64
SSD_GROUPS = 2
SSD_STATE = 128
TOP_K = 4
SWIGLU_LIMIT = 7.0
SWIGLU_ALPHA = 1.702

LANES = 128
VMEM_LIMIT = 56 * 1024 * 1024
NEG_BIG = -1e30
MOE_TILE = 256

SM_AF, SM_AB, SM_DTF, SM_DTB = 0, 16, 32, 48


def _cparams(sem):
    return pltpu.CompilerParams(dimension_semantics=sem, vmem_limit_bytes=VMEM_LIMIT)


def _pick(n, pref, mult=8):
    if n <= pref:
        return n
    t = (pref // mult) * mult
    while t >= mult:
        if n % t == 0:
            return t
        t -= mult
    return n


def _dot(a, b):
    return jnp.dot(a.astype(BF16), b.astype(BF16), preferred_element_type=F32)


def _dot_nt(a, b):
    return lax.dot_general(a.astype(BF16), b.astype(BF16), (((1,), (1,)), ((), ())),
                           preferred_element_type=F32)


def _dot_tn(a, b):
    return lax.dot_general(a.astype(BF16), b.astype(BF16), (((0,), (0,)), ((), ())),
                           preferred_element_type=F32)


def _split2(x):
    hi = x.astype(BF16)
    lo = (x - hi.astype(F32)).astype(BF16)
    return hi, lo


def _split3(x):
    hi = x.astype(BF16)
    r = x - hi.astype(F32)
    mid = r.astype(BF16)
    lo = (r - mid.astype(F32)).astype(BF16)
    return hi, mid, lo


def _sel_dot(sel, x):
    hi, mid, lo = _split3(x)
    dot = lambda p: jnp.dot(sel, p, preferred_element_type=F32)
    return dot(hi) + dot(mid) + dot(lo)


def _dot_sel(x, sel):
    hi, lo = _split2(x)
    return (jnp.dot(hi, sel, preferred_element_type=F32)
            + jnp.dot(lo, sel, preferred_element_type=F32))


def _sigmoid(x):
    return 1.0 / (1.0 + jnp.exp(-x))


def _silu(x):
    return x * _sigmoid(x)


def _softplus(x):
    return jnp.maximum(x, 0.0) + jnp.log(1.0 + jnp.exp(-jnp.abs(x)))


def _log_sigmoid(x):
    return jnp.minimum(x, 0.0) - jnp.log(1.0 + jnp.exp(-jnp.abs(x)))


def _mod_kernel(c_ref, w_ref, b_ref, o_ref):
    a = _silu(c_ref[...])
    o_ref[...] = _dot(a, w_ref[...]) + b_ref[...]


def _mod_call(cin, w_mod, b_mod):
    nl, d, n = w_mod.shape
    tn = _pick(n, 512, LANES)
    return pl.pallas_call(
        _mod_kernel,
        grid=(nl, n // tn),
        in_specs=[pl.BlockSpec((8, d), lambda l, j: (0, 0)),
                  pl.BlockSpec((None, d, tn), lambda l, j: (l, 0, j)),
                  pl.BlockSpec((None, 1, tn), lambda l, j: (l, 0, j))],
        out_specs=pl.BlockSpec((None, 8, tn), lambda l, j: (l, 0, j)),
        out_shape=jax.ShapeDtypeStruct((nl, 8, n), F32),
        compiler_params=_cparams(("arbitrary", "arbitrary")),
        name="mod_vectors",
    )(cin, w_mod, b_mod)


def _norm_mod(x, g, sc, sh):
    y = x * lax.rsqrt(jnp.mean(x * x, axis=-1, keepdims=True) + EPS) * g
    return y * (1.0 + sc) + sh


def _norm_kernel(x_ref, g_ref, sc_ref, sh_ref, o_ref):
    o_ref[...] = _norm_mod(x_ref[...], g_ref[...], sc_ref[...], sh_ref[...]).astype(o_ref.dtype)


def _norm_call(x, g, sc, sh, rows_per_group, out_dtype):
    r, d = x.shape
    tm = _pick(rows_per_group, 256)
    per = rows_per_group // tm
    ng = sc.shape[0]
    gidx = (lambda i: (i // per, 0, 0)) if ng > 1 else (lambda i: (0, 0, 0))
    return pl.pallas_call(
        _norm_kernel,
        grid=(r // tm,),
        in_specs=[pl.BlockSpec((tm, d), lambda i: (i, 0)),
                  pl.BlockSpec((1, d), lambda i: (0, 0)),
                  pl.BlockSpec((None, 1, d), gidx),
                  pl.BlockSpec((None, 1, d), gidx)],
        out_specs=pl.BlockSpec((tm, d), lambda i: (i, 0)),
        out_shape=jax.ShapeDtypeStruct((r, d), out_dtype),
        compiler_params=_cparams(("arbitrary",)),
        name="norm_mod",
    )(x, g, sc, sh)


def _mm_kernel(a_ref, b_ref, o_ref):
    o_ref[...] = jnp.dot(a_ref[...], b_ref[...], preferred_element_type=F32)


def _mm_call(a, b, tm_pref, tn):
    r, k = a.shape
    n = b.shape[1]
    tm = _pick(r, tm_pref)
    assert n % tn == 0
    return pl.pallas_call(
        _mm_kernel,
        grid=(n // tn, r // tm),
        in_specs=[pl.BlockSpec((tm, k), lambda j, i: (i, 0)),
                  pl.BlockSpec((k, tn), lambda j, i: (0, j))],
        out_specs=pl.BlockSpec((tm, tn), lambda j, i: (i, j)),
        out_shape=jax.ShapeDtypeStruct((r, n), F32),
        compiler_params=_cparams(("arbitrary", "arbitrary")),
        name="in_proj",
    )(a, b)


def _conv_kernel(seg_len, glu_ref, w_ref, b_ref, lg_ref, lb_ref, o_ref):
    blk, ch = o_ref.shape
    kw = w_ref.shape[0]
    half = kw // 2
    u = glu_ref[:, :ch] * _sigmoid(glu_ref[:, ch:])
    pos = lax.broadcasted_iota(jnp.int32, (blk, 1), 0) % seg_len
    acc = jnp.zeros((blk, ch), F32)
    for j in range(kw):
        d = j - half
        shifted = u if d == 0 else pltpu.roll(u, (-d) % blk, 0)
        valid = jnp.logical_and(pos + d >= 0, pos + d < seg_len)
        acc = acc + jnp.where(valid, shifted, 0.0) * w_ref[j:j + 1, :]
    y = acc + b_ref[...]
    mu = jnp.mean(y, axis=-1, keepdims=True)
    yc = y - mu
    var = jnp.mean(yc * yc, axis=-1, keepdims=True)
    z = yc * lax.rsqrt(var + EPS) * lg_ref[...] + lb_ref[...]
    o_ref[...] = _silu(z).astype(o_ref.dtype)


def _conv_call(proj, conv_w, conv_b, ln_g, ln_b, seg_len):
    r = proj.shape[0]
    kw, ch = conv_w.shape
    blk = seg_len * max(1, 256 // seg_len)
    assert r % blk == 0
    vec = lambda: pl.BlockSpec((1, ch), lambda i: (0, 0))
    return pl.pallas_call(
        functools.partial(_conv_kernel, seg_len),
        grid=(r // blk,),
        in_specs=[pl.BlockSpec((blk, 2 * ch), lambda i: (i, 0)),
                  pl.BlockSpec((kw, ch), lambda i: (0, 0)),
                  vec(), vec(), vec()],
        out_specs=pl.BlockSpec((blk, ch), lambda i: (i, 0)),
        out_shape=jax.ShapeDtypeStruct((r, ch), BF16),
        compiler_params=_cparams(("arbitrary",)),
        name="conformer_conv",
    )(proj, conv_w, conv_b, ln_g, ln_b)


def _ssd_prep_kernel(nblk, cur_ref, prev_ref, next_ref, w_ref, b_ref, o_ref):
    blk, ch = o_ref.shape
    kw = w_ref.shape[0]
    half = kw // 2
    i = pl.program_id(1)
    prev = jnp.where(i > 0, prev_ref[...], 0.0)
    nxt = jnp.where(i < nblk - 1, next_ref[...], 0.0)
    ext = jnp.concatenate([prev, cur_ref[...], nxt], axis=0)
    n = blk + 16
    acc = jnp.zeros((blk, ch), F32)
    for j in range(kw):
        d = j - half
        shifted = ext if d == 0 else pltpu.roll(ext, (-d) % n, 0)
        acc = acc + shifted[8:8 + blk, :] * w_ref[j:j + 1, :]
    o_ref[...] = _silu(acc + b_ref[...])


def _ssd_prep_call(proj, col_blk, width, w, b, nseq, seq_len):
    r = proj.shape[0]
    kw = w.shape[0]
    blk = _pick(seq_len, 256)
    nblk = seq_len // blk
    b8 = blk // 8
    nb8 = r // 8
    cur = lambda s, i: (s * nblk + i, col_blk)
    prev = lambda s, i: (jnp.maximum((s * nblk + i) * b8 - 1, 0), col_blk)
    nxt = lambda s, i: (jnp.minimum((s * nblk + i + 1) * b8, nb8 - 1), col_blk)
    return pl.pallas_call(
        functools.partial(_ssd_prep_kernel, nblk),
        grid=(nseq, nblk),
        in_specs=[pl.BlockSpec((blk, width), cur),
                  pl.BlockSpec((8, width), prev),
                  pl.BlockSpec((8, width), nxt),
                  pl.BlockSpec((kw, width), lambda s, i: (0, 0)),
                  pl.BlockSpec((1, width), lambda s, i: (0, 0))],
        out_specs=pl.BlockSpec((blk, width), lambda s, i: (s * nblk + i, 0)),
        out_shape=jax.ShapeDtypeStruct((r, width), F32),
        compiler_params=_cparams(("arbitrary", "arbitrary")),
        name="ssd_conv",
    )(proj, proj, proj, w, b)


def _gla_kernel(rev, last, nblk, *refs):
    if last:
        (q_ref, k_ref, v_ref, sm_ref, wa_ref, ba_ref, tri_ref, s0_ref, of_ref, r_ref, g_ref,
         o_ref, sfin_ref, st_ref, sp_ref) = refs
    else:
        (q_ref, k_ref, v_ref, sm_ref, wa_ref, ba_ref, tri_ref, s0_ref,
         o_ref, sfin_ref, st_ref, sp_ref) = refs
    i = pl.program_id(2)

    @pl.when(i == 0)
    def _():
        st_ref[...] = s0_ref[...]

    blk = q_ref.shape[0]
    nch = blk // CHUNK
    z = _dot(sm_ref[...], wa_ref[...]) + ba_ref[...]
    lg = _log_sigmoid(z) * (1.0 / GLA_TAU)
    g = _sel_dot(tri_ref[...], lg)
    g3 = g.reshape(nch, CHUNK, GLA_DK)
    ref_i = CHUNK // 2 - 1 if rev else CHUNK // 2
    last_i = 0 if rev else CHUNK - 1
    bcast = lambda t: jnp.broadcast_to(t, (nch, CHUNK, GLA_DK)).reshape(blk, GLA_DK)
    g_ref_pt = bcast(g3[:, ref_i:ref_i + 1, :])
    g_last3 = g3[:, last_i:last_i + 1, :]
    g_last = bcast(g_last3)

    q = q_ref[...] * (GLA_DK ** -0.5)
    k = k_ref[...]
    v = v_ref[...].astype(BF16)
    qe = q * jnp.exp(g - g_ref_pt)
    ke = k * jnp.exp(g_ref_pt - g)
    att = _dot_nt(qe, ke)
    row = lax.broadcasted_iota(jnp.int32, (blk, blk), 0)
    col = lax.broadcasted_iota(jnp.int32, (blk, blk), 1)
    same = (row // CHUNK) == (col // CHUNK)
    tri = (row <= col) if rev else (row >= col)
    att = jnp.where(jnp.logical_and(same, tri), att, 0.0)
    y = _dot(att, v)

    kd = (k * jnp.exp(g_last - g)).astype(BF16)
    qg = (q * jnp.exp(g)).astype(BF16)
    dec = jnp.exp(g_last3)
    st = st_ref[...]
    order = range(nch - 1, -1, -1) if rev else range(nch)
    for c in order:
        sl = slice(c * CHUNK, (c + 1) * CHUNK)
        sp_ref[c] = st
        st = dec[c] * st + _dot_tn(v[sl, :], kd[sl, :])
    st_ref[...] = st
    y_off = [_dot_nt(qg[c * CHUNK:(c + 1) * CHUNK, :], sp_ref[c]) for c in range(nch)]
    y = y + jnp.concatenate(y_off, axis=0)

    @pl.when(i == nblk - 1)
    def _():
        sfin_ref[...] = st

    if last:
        o = of_ref[...] + y
        o = o * lax.rsqrt(jnp.mean(o * o, axis=-1, keepdims=True) + EPS) * g_ref[...]
        o_ref[...] = (o * _silu(r_ref[...])).astype(o_ref.dtype)
    else:
        o_ref[...] = y


def _gla_call(rev, proj, cols, wa_pad, ba, tri, s0, nseq, seq_len, heads, extra=None):
    r = proj.shape[0]
    blk = tri.shape[0]
    nblk = seq_len // blk
    last = extra is not None
    tok = (lambda s, i: s * nblk + (nblk - 1 - i)) if rev else (lambda s, i: s * nblk + i)
    colspec = lambda off: pl.BlockSpec((blk, LANES), lambda s, h, i: (tok(s, i), off + h))
    hvec = pl.BlockSpec((1, LANES), lambda s, h, i: (0, h))
    state = pl.BlockSpec((None, None, GLA_DV, GLA_DK), lambda s, h, i: (s, h, 0, 0))
    in_specs = [colspec(cols["q"]), colspec(cols["k"]), colspec(cols["v"]),
                pl.BlockSpec((blk, LANES), lambda s, h, i: (tok(s, i), cols["small"])),
                pl.BlockSpec((LANES, LANES), lambda s, h, i: (0, h)),
                hvec,
                pl.BlockSpec((blk, blk), lambda s, h, i: (0, 0)),
                state]
    args = [proj, proj, proj, proj, wa_pad, ba, tri, s0]
    if last:
        o_f, norm_g = extra
        in_specs += [pl.BlockSpec((blk, LANES), lambda s, h, i: (tok(s, i), h)),
                     colspec(cols["r"]), hvec]
        args += [o_f, proj, norm_g]
    out_dtype = BF16 if last else F32
    return pl.pallas_call(
        functools.partial(_gla_kernel, rev, last, nblk),
        grid=(nseq, heads, nblk),
        in_specs=in_specs,
        out_specs=[pl.BlockSpec((blk, LANES), lambda s, h, i: (tok(s, i), h)), state],
        out_shape=[jax.ShapeDtypeStruct((r, heads * GLA_DV), out_dtype),
                   jax.ShapeDtypeStruct((nseq, heads, GLA_DV, GLA_DK), F32)],
        scratch_shapes=[pltpu.VMEM((GLA_DV, GLA_DK), F32),
                        pltpu.VMEM((blk // CHUNK, GLA_DV, GLA_DK), F32)],
        compiler_params=_cparams(("arbitrary", "arbitrary", "arbitrary")),
        name="gla_bwd" if rev else "gla_fwd",
    )(*args)


def _ssd_kernel(rev, last, nblk, lane0, *refs):
    if last:
        (xs_ref, bc_ref, sm_ref, dtb_ref, a_ref, tri_ref, ex_ref, h0_ref, yf_ref, z_ref, d_ref,
         ng_ref, o_ref, hfin_ref, h_ref) = refs
    else:
        (xs_ref, bc_ref, sm_ref, dtb_ref, a_ref, tri_ref, ex_ref, h0_ref,
         o_ref, hfin_ref, h_ref) = refs
    i = pl.program_id(1)

    @pl.when(i == 0)
    def _():
        h_ref[...] = h0_ref[...]

    blk, inner = xs_ref.shape
    gsz = inner // SSD_GROUPS
    epg = gsz // SSD_HEADDIM
    ns = SSD_STATE

    dt = _softplus(sm_ref[...] + dtb_ref[...])
    ad = a_ref[...] * dt
    cum = _sel_dot(tri_ref[...], ad)
    tot = cum[0:1, :] if rev else cum[blk - 1:blk, :]
    cum_t = cum.T
    ex = ex_ref[...]
    xs = xs_ref[...]
    xd = (xs * _dot_sel(dt, ex)).astype(BF16)
    xdw = (xs * _dot_sel(dt * jnp.exp(tot - cum), ex)).astype(BF16)
    ecum = _dot_sel(jnp.exp(cum), ex)

    row = lax.broadcasted_iota(jnp.int32, (blk, blk), 0)
    col = lax.broadcasted_iota(jnp.int32, (blk, blk), 1)
    causal = (row <= col) if rev else (row >= col)
    lane = lax.broadcasted_iota(jnp.int32, (blk, LANES), 1)
    low = lane < SSD_HEADDIM

    y_cols = []
    for g in range(SSD_GROUPS):
        bm = bc_ref[:, g * ns:(g + 1) * ns].astype(BF16)
        cm = bc_ref[:, (SSD_GROUPS + g) * ns:(SSD_GROUPS + g + 1) * ns].astype(BF16)
        cb = _dot_nt(cm, bm)
        hg = h_ref[g]
        y_off = _dot_nt(cm, hg)
        st_new = _dot_tn(xdw[:, g * gsz:(g + 1) * gsz], bm)
        for pr in range(epg // 2):
            c0 = g * gsz + pr * LANES
            xpair = xd[:, c0:c0 + LANES]
            acc = y_off[:, pr * LANES:(pr + 1) * LANES] * ecum[:, c0:c0 + LANES]
            for half in range(2):
                ln = lane0 + g * epg + 2 * pr + half
                seg = cum[:, ln:ln + 1] - cum_t[ln:ln + 1, :]
                w = (cb * jnp.exp(jnp.where(causal, seg, NEG_BIG))).astype(BF16)
                keep = low if half == 0 else jnp.logical_not(low)
                rhs = jnp.where(keep, xpair, jnp.zeros_like(xpair))
                acc = acc + jnp.dot(w, rhs, preferred_element_type=F32)
            y_cols.append(acc)
        for e in range(epg):
            ln = lane0 + g * epg + e
            rs = slice(e * SSD_HEADDIM, (e + 1) * SSD_HEADDIM)
            dec = jnp.broadcast_to(jnp.exp(tot[:, ln:ln + 1]), (SSD_HEADDIM, ns))
            h_ref[g, rs, :] = dec * hg[rs, :] + st_new[rs, :]
    y = jnp.concatenate(y_cols, axis=1)

    @pl.when(i == nblk - 1)
    def _():
        hfin_ref[...] = h_ref[...]

    if last:
        ys = (yf_ref[...] + y + d_ref[...] * xs) * _silu(z_ref[...])
        outs = []
        for g in range(SSD_GROUPS):
            yg = ys[:, g * gsz:(g + 1) * gsz]
            outs.append(yg * lax.rsqrt(jnp.mean(yg * yg, axis=-1, keepdims=True) + EPS))
        o_ref[...] = (jnp.concatenate(outs, axis=1) * ng_ref[...]).astype(o_ref.dtype)
    else:
        o_ref[...] = y


def _ssd_call(rev, xs, bc, proj, cols, dtb, a_row, tri, ex, h0, nseq, seq_len, extra=None):
    r, inner = xs.shape
    blk = tri.shape[0]
    nblk = seq_len // blk
    last = extra is not None
    gsz = inner // SSD_GROUPS
    lane0 = SM_DTB if rev else SM_DTF
    tok = (lambda s, i: s * nblk + (nblk - 1 - i)) if rev else (lambda s, i: s * nblk + i)
    rowblk = lambda w: pl.BlockSpec((blk, w), lambda s, i: (tok(s, i), 0))
    const = lambda shp: pl.BlockSpec(shp, lambda s, i: (0,) * len(shp))
    state = pl.BlockSpec((None, SSD_GROUPS, gsz, SSD_STATE), lambda s, i: (s, 0, 0, 0))
    in_specs = [rowblk(inner), rowblk(bc.shape[1]),
                pl.BlockSpec((blk, LANES), lambda s, i: (tok(s, i), cols["small"])),
                const((1, LANES)), const((1, LANES)), const((blk, blk)), const((LANES, inner)),
                state]
    args = [xs, bc, proj, dtb, a_row, tri, ex, h0]
    if last:
        y_f, dskip, norm_g = extra
        in_specs += [rowblk(inner),
                     pl.BlockSpec((blk, inner), lambda s, i: (tok(s, i), cols["z"])),
                     const((1, inner)), const((1, inner))]
        args += [y_f, proj, dskip, norm_g]
    return pl.pallas_call(
        functools.partial(_ssd_kernel, rev, last, nblk, lane0),
        grid=(nseq, nblk),
        in_specs=in_specs,
        out_specs=[rowblk(inner), state],
        out_shape=[jax.ShapeDtypeStruct((r, inner), BF16 if last else F32),
                   jax.ShapeDtypeStruct((nseq, SSD_GROUPS, gsz, SSD_STATE), F32)],
        scratch_shapes=[pltpu.VMEM((SSD_GROUPS, gsz, SSD_STATE), F32)],
        compiler_params=_cparams(("arbitrary", "arbitrary")),
        name="ssd_bwd" if rev else "ssd_fwd",
    )(*args)


def _merge_kernel(gc_ref, gg_ref, gs_ref, yc_ref, yg_ref, ys_ref, wc_ref, wg_ref, ws_ref, o_ref):
    dot = lambda a, b: jnp.dot(a[...], b[...], preferred_element_type=F32)
    acc = _sigmoid(gc_ref[...]) * dot(yc_ref, wc_ref)
    acc = acc + _sigmoid(gg_ref[...]) * dot(yg_ref, wg_ref)
    acc = acc + _sigmoid(gs_ref[...]) * dot(ys_ref, ws_ref)
    o_ref[...] = acc.astype(o_ref.dtype)


def _merge_call(proj, gate_col, y_conv, y_gla, y_ssd, w_c, w_g, w_s, tn):
    r = proj.shape[0]
    d = w_c.shape[1]
    tm = _pick(r, 512)
    nj = d // tn
    gate = lambda b: pl.BlockSpec((tm, tn), lambda j, i: (i, gate_col + b * nj + j))
    yspec = lambda y: pl.BlockSpec((tm, y.shape[1]), lambda j, i: (i, 0))
    wspec = lambda w: pl.BlockSpec((w.shape[0], tn), lambda j, i: (0, j))
    return pl.pallas_call(
        _merge_kernel,
        grid=(nj, r // tm),
        in_specs=[gate(0), gate(1), gate(2), yspec(y_conv), yspec(y_gla), yspec(y_ssd),
                  wspec(w_c), wspec(w_g), wspec(w_s)],
        out_specs=pl.BlockSpec((tm, tn), lambda j, i: (i, j)),
        out_shape=jax.ShapeDtypeStruct((r, d), BF16),
        compiler_params=_cparams(("arbitrary", "arbitrary")),
        name="gated_merge",
    )(proj, proj, proj, y_conv, y_gla, y_ssd, w_c, w_g, w_s)


def _out_kernel(m_ref, w_ref, x_ref, gt_ref, o_ref):
    o_ref[...] = x_ref[...] + gt_ref[...] * jnp.dot(m_ref[...], w_ref[...],
                                                     preferred_element_type=F32)


def _out_call(merged, w_out, x, gate, rows_per_group):
    r, d = x.shape
    tm = _pick(rows_per_group, 512)
    tn = _pick(d, 1024, LANES)
    per = rows_per_group // tm
    gidx = (lambda j, i: (i // per, 0, j)) if gate.shape[0] > 1 else (lambda j, i: (0, 0, j))
    return pl.pallas_call(
        _out_kernel,
        grid=(d // tn, r // tm),
        in_specs=[pl.BlockSpec((tm, d), lambda j, i: (i, 0)),
                  pl.BlockSpec((d, tn), lambda j, i: (0, j)),
                  pl.BlockSpec((tm, tn), lambda j, i: (i, j)),
                  pl.BlockSpec((None, 1, tn), gidx)],
        out_specs=pl.BlockSpec((tm, tn), lambda j, i: (i, j)),
        out_shape=jax.ShapeDtypeStruct((r, d), F32),
        compiler_params=_cparams(("arbitrary", "arbitrary")),
        name="out_proj",
    )(merged, w_out, x, gate)


def _router_kernel(x_ref, g_ref, sc_ref, sh_ref, wr_ref, br_ref, h_ref, idx_ref, w_ref):
    h = _norm_mod(x_ref[...], g_ref[...], sc_ref[...], sh_ref[...])
    h_ref[...] = h
    hhi, hlo = _split2(h)
    whi, wmid, wlo = wr_ref[0], wr_ref[1], wr_ref[2]
    dot = lambda a, b: jnp.dot(a, b, preferred_element_type=F32)
    logits = (dot(hhi, whi) + dot(hhi, wmid) + dot(hlo, whi) + dot(hhi, wlo) + dot(hlo, wmid)
              + br_ref[...])
    tm, ne = logits.shape
    lane = lax.broadcasted_iota(jnp.int32, (tm, ne), 1).astype(F32)
    work = logits
    idxs, vals = [], []
    for _ in range(TOP_K):
        m = jnp.max(work, axis=-1, keepdims=True)
        idx = jnp.min(jnp.where(work == m, lane, float(ne)), axis=-1, keepdims=True)
        idxs.append(idx)
        vals.append(m)
        work = jnp.where(lane == idx, -jnp.inf, work)
    es = [jnp.exp(v - vals[0]) for v in vals]
    inv = 1.0 / sum(es)
    out_lane = lax.broadcasted_iota(jnp.int32, (tm, LANES), 1)
    idx_out = jnp.zeros((tm, LANES), F32)
    w_out = jnp.zeros((tm, LANES), F32)
    for k in range(TOP_K):
        idx_out = jnp.where(out_lane == k, idxs[k], idx_out)
        w_out = jnp.where(out_lane == k, es[k] * inv, w_out)
    idx_ref[...] = idx_out.astype(jnp.int32)
    w_ref[...] = w_out


def _router_call(x, g, sc, sh, wr3, br, rows_per_group):
    r, d = x.shape
    ne = wr3.shape[2]
    tm = _pick(rows_per_group, 256)
    per = rows_per_group // tm
    gidx = (lambda i: (i // per, 0, 0)) if sc.shape[0] > 1 else (lambda i: (0, 0, 0))
    row = lambda w: pl.BlockSpec((tm, w), lambda i: (i, 0))
    return pl.pallas_call(
        _router_kernel,
        grid=(r // tm,),
        in_specs=[row(d),
                  pl.BlockSpec((1, d), lambda i: (0, 0)),
                  pl.BlockSpec((None, 1, d), gidx),
                  pl.BlockSpec((None, 1, d), gidx),
                  pl.BlockSpec((3, d, ne), lambda i: (0, 0, 0)),
                  pl.BlockSpec((1, ne), lambda i: (0, 0))],
        out_specs=[row(d), row(LANES), row(LANES)],
        out_shape=[jax.ShapeDtypeStruct((r, d), F32), jax.ShapeDtypeStruct((r, LANES), jnp.int32),
                   jax.ShapeDtypeStruct((r, LANES), F32)],
        compiler_params=_cparams(("arbitrary",)),
        name="router",
    )(x, g, sc, sh, wr3, br)


def _dispatch_plan(top_i, ne, tm):
    n, topk = top_i.shape
    n_tiles = (n * topk) // tm + ne
    onehot = (top_i[:, :, None] == jnp.arange(ne, dtype=jnp.int32)[None, None, :]).any(axis=1)
    onehot = onehot.astype(jnp.int32)
    cnt = onehot.sum(axis=0)
    rank = jnp.cumsum(onehot, axis=0) - onehot
    tiles_e = (cnt + tm - 1) // tm
    tile_end = jnp.cumsum(tiles_e)
    tile_start = tile_end - tiles_e
    n_valid = tile_end[-1]
    slot = tile_start[top_i] * tm + jnp.take_along_axis(rank, top_i, axis=1)
    tok = jnp.broadcast_to(jnp.arange(n, dtype=jnp.int32)[:, None], (n, topk))
    dst = tok + jnp.arange(topk, dtype=jnp.int32)[None, :] * n
    flat = slot.reshape(-1)
    tok_of_slot = jnp.zeros((n_tiles * tm,), jnp.int32).at[flat].set(tok.reshape(-1))
    dst_of_slot = jnp.zeros((n_tiles * tm,), jnp.int32).at[flat].set(dst.reshape(-1))
    t = jnp.arange(n_tiles, dtype=jnp.int32)
    te = jnp.minimum(jnp.searchsorted(tile_end, t, side="right").astype(jnp.int32), ne - 1)
    n_real = jnp.clip(cnt[te] - (t - tile_start[te]) * tm, 0, tm).astype(jnp.int32)
    return (te, n_real, n_valid.astype(jnp.int32).reshape(1),
            tok_of_slot.reshape(n_tiles, 1, tm), dst_of_slot.reshape(n_tiles, 1, tm))


def _experts_kernel(tm, te_ref, nr_ref, nv_ref, tok_ref, tokn_ref, dst_ref, h_hbm, wi_ref, bi_ref,
                    wo_ref, bo_ref, y_hbm, gbuf, ybuf, gsem, ssem):
    t = pl.program_id(0)
    nv = nv_ref[0]
    slot = t % 2

    def row_gather(ids_ref, s, r):
        return pltpu.make_async_copy(h_hbm.at[pl.ds(ids_ref[0, r], 1)], gbuf.at[s, pl.ds(r, 1)],
                                     gsem.at[s])

    def row_scatter(s, r, dst_row):
        return pltpu.make_async_copy(ybuf.at[s, pl.ds(r, 1)], y_hbm.at[pl.ds(dst_row, 1)],
                                     ssem.at[s])

    def start_gather(ids_ref, s):
        def body(r, c):
            row_gather(ids_ref, s, r).start()
            return c
        lax.fori_loop(0, tm, body, 0, unroll=8)

    def wait_scatter(s, n):
        def body(r, c):
            row_scatter(s, 0, 0).wait()
            return c
        lax.fori_loop(0, n, body, 0)

    @pl.when(t == 0)
    def _():
        start_gather(tok_ref, 0)

    @pl.when(t + 1 < nv)
    def _():
        start_gather(tokn_ref, 1 - slot)

    @pl.when(t < nv)
    def _():
        pltpu.make_async_copy(h_hbm.at[pl.ds(0, tm)], gbuf.at[slot], gsem.at[slot]).wait()

        @pl.when(t >= 2)
        def _():
            wait_scatter(slot, nr_ref[t - 2])

        ff = wo_ref.shape[0]
        gu = jnp.dot(gbuf[slot].astype(BF16), wi_ref[...], preferred_element_type=F32) + bi_ref[...]
        gate = jnp.minimum(gu[:, :ff], SWIGLU_LIMIT)
        up = jnp.clip(gu[:, ff:], -SWIGLU_LIMIT, SWIGLU_LIMIT)
        act = gate * _sigmoid(SWIGLU_ALPHA * gate) * (up + 1.0)
        ybuf[slot] = jnp.dot(act.astype(BF16), wo_ref[...], preferred_element_type=F32) + bo_ref[...]

        def body(r, c):
            row_scatter(slot, r, dst_ref[0, r]).start()
            return c
        lax.fori_loop(0, nr_ref[t], body, 0)

        @pl.when(t == nv - 1)
        def _():
            wait_scatter(slot, nr_ref[t])

            @pl.when(t >= 1)
            def _():
                wait_scatter(1 - slot, nr_ref[t - 1])


def _experts_call(h, plan, w_in, b_in, w_out, b_out, tm):
    n, d = h.shape
    te, n_real, n_valid, tok, dst = plan
    n_tiles = te.shape[0]
    ne, _, ff2 = w_in.shape
    ff = ff2 // 2
    ids = lambda f: pl.BlockSpec((None, 1, tm), f, memory_space=pltpu.SMEM)
    grid_spec = pltpu.PrefetchScalarGridSpec(
        num_scalar_prefetch=3,
        grid=(n_tiles,),
        in_specs=[ids(lambda t, te, nr, nv: (t, 0, 0)),
                  ids(lambda t, te, nr, nv: (jnp.minimum(t + 1, n_tiles - 1), 0, 0)),
                  ids(lambda t, te, nr, nv: (t, 0, 0)),
                  pl.BlockSpec(memory_space=pl.ANY),
                  pl.BlockSpec((None, d, ff2), lambda t, te, nr, nv: (te[t], 0, 0)),
                  pl.BlockSpec((None, 1, ff2), lambda t, te, nr, nv: (te[t], 0, 0)),
                  pl.BlockSpec((None, ff, d), lambda t, te, nr, nv: (te[t], 0, 0)),
                  pl.BlockSpec((None, 1, d), lambda t, te, nr, nv: (te[t], 0, 0))],
        out_specs=pl.BlockSpec(memory_space=pl.ANY),
        scratch_shapes=[pltpu.VMEM((2, tm, d), F32), pltpu.VMEM((2, tm, d), F32),
                        pltpu.SemaphoreType.DMA((2,)), pltpu.SemaphoreType.DMA((2,))],
    )
    return pl.pallas_call(
        functools.partial(_experts_kernel, tm),
        grid_spec=grid_spec,
        out_shape=jax.ShapeDtypeStruct((TOP_K * n, d), F32),
        compiler_params=_cparams(("arbitrary",)),
        name="moe_experts",
    )(te, n_real, n_valid, tok, tok, dst, h, w_in, b_in, w_out, b_out)


def _combine_kernel(final, y_ref, w_ref, x_ref, gt_ref, *rest):
    if final:
        gf_ref, o_ref = rest
    else:
        (o_ref,) = rest
    w = w_ref[...]
    acc = w[:, 0:1] * y_ref[0]
    for k in range(1, TOP_K):
        acc = acc + w[:, k:k + 1] * y_ref[k]
    o = x_ref[...] + gt_ref[...] * acc
    if final:
        o = o * lax.rsqrt(jnp.mean(o * o, axis=-1, keepdims=True) + EPS) * gf_ref[...]
    o_ref[...] = o


def _combine_call(y, top_w, x, gate, rows_per_group, g_final=None):
    r, d = x.shape
    tm = _pick(rows_per_group, 128)
    per = rows_per_group // tm
    gidx = (lambda i: (i // per, 0, 0)) if gate.shape[0] > 1 else (lambda i: (0, 0, 0))
    final = g_final is not None
    in_specs = [pl.BlockSpec((TOP_K, tm, d), lambda i: (0, i, 0)),
                pl.BlockSpec((tm, LANES), lambda i: (i, 0)),
                pl.BlockSpec((tm, d), lambda i: (i, 0)),
                pl.BlockSpec((None, 1, d), gidx)]
    args = [y.reshape(TOP_K, r, d), top_w, x, gate]
    if final:
        in_specs.append(pl.BlockSpec((1, d), lambda i: (0, 0)))
        args.append(g_final)
    return pl.pallas_call(
        functools.partial(_combine_kernel, final),
        grid=(r // tm,),
        in_specs=in_specs,
        out_specs=pl.BlockSpec((tm, d), lambda i: (i, 0)),
        out_shape=jax.ShapeDtypeStruct((r, d), F32),
        compiler_params=_cparams(("arbitrary",)),
        name="moe_combine",
    )(*args)


def _tri_blockdiag(blk, chunk, upper):
    i = jnp.arange(blk)
    same = (i[:, None] // chunk) == (i[None, :] // chunk)
    tri = (i[:, None] <= i[None, :]) if upper else (i[:, None] >= i[None, :])
    return jnp.logical_and(same, tri).astype(BF16)


def _layer_params(l, p, dims):
    d, ch, qk, inner, heads_ssd = dims["d"], dims["ch"], dims["qk"], dims["inner"], dims["ssd_heads"]
    w_in = p["w_in"][l]
    xbc = inner + 2 * SSD_GROUPS * SSD_STATE
    sizes = [("glu", 2 * ch), ("q", qk), ("k", qk), ("v", qk), ("r", qk), ("a_f", GLA_RANK),
             ("a_b", GLA_RANK), ("z", inner), ("xbc", xbc), ("dt_f", heads_ssd),
             ("dt_b", heads_ssd), ("gate", N_BRANCH * d)]
    off, o = {}, 0
    for name, s in sizes:
        off[name] = (o, s)
        o += s
    col = lambda name, a=0, b=None: w_in[:, off[name][0] + a: off[name][0] + (off[name][1] if b is None else b)]
    small = jnp.zeros((d, LANES), F32)
    small = small.at[:, SM_AF:SM_AF + GLA_RANK].set(col("a_f"))
    small = small.at[:, SM_AB:SM_AB + GLA_RANK].set(col("a_b"))
    small = small.at[:, SM_DTF:SM_DTF + heads_ssd].set(col("dt_f"))
    small = small.at[:, SM_DTB:SM_DTB + heads_ssd].set(col("dt_b"))
    parts = [col("glu"), col("q"), col("k"), col("v"), col("r"), col("z"),
             col("xbc", 0, inner), col("xbc", inner), col("gate"), small]
    w_r = jnp.concatenate(parts, axis=1)
    n_real = w_r.shape[1]
    tn = dims["tn_in"]
    n_pad = -(-n_real // tn) * tn
    w_r = jnp.pad(w_r, ((0, 0), (0, n_pad - n_real))).astype(BF16)

    def wa_pad(wa, lane0):
        return jnp.zeros((LANES, qk), F32).at[lane0:lane0 + GLA_RANK].set(wa).astype(BF16)

    def lane_row(vec, lane0):
        return jnp.zeros((1, LANES), F32).at[0, lane0:lane0 + heads_ssd].set(vec)

    conv_w = p["ssd_conv_w"][l]
    conv_b = p["ssd_conv_b"][l][None, :]
    wr = p["w_router"][l]
    whi = wr.astype(BF16)
    r1 = wr - whi.astype(F32)
    wmid = r1.astype(BF16)
    wlo = (r1 - wmid.astype(F32)).astype(BF16)
    return dict(
        w_in=w_r,
        conv_w=p["conv_w"][l], conv_b=p["conv_b"][l][None], ln_g=p["conv_ln_g"][l][None],
        ln_b=p["conv_ln_b"][l][None],
        wa_f=wa_pad(p["gla_wa_f"][l], SM_AF), wa_b=wa_pad(p["gla_wa_b"][l], SM_AB),
        ba_f=p["gla_ba_f"][l][None], ba_b=p["gla_ba_b"][l][None],
        gla_g=p["gla_norm_g"][l][None],
        cw_x=conv_w[:, :inner], cw_bc=conv_w[:, inner:], cb_x=conv_b[:, :inner], cb_bc=conv_b[:, inner:],
        dtb=lane_row(p["ssd_dt_bias_f"][l], SM_DTF) + lane_row(p["ssd_dt_bias_b"][l], SM_DTB),
        a_f=lane_row(-jnp.exp(p["ssd_a_log_f"][l]), SM_DTF),
        a_b=lane_row(-jnp.exp(p["ssd_a_log_b"][l]), SM_DTB),
        dskip=jnp.repeat(p["ssd_d"][l], SSD_HEADDIM)[None],
        ssd_g=p["ssd_norm_g"][l][None],
        w_c=p["w_proj_conv"][l].astype(BF16), w_g=p["w_proj_gla"][l].astype(BF16),
        w_s=p["w_proj_ssd"][l].astype(BF16), w_out=p["w_out"][l].astype(BF16),
        wr3=jnp.stack([whi, wmid, wlo]), br=p["b_router"][l][None],
        we_in=p["w_exp_in"][l].astype(BF16), be_in=p["b_exp_in"][l][:, None, :],
        we_out=p["w_exp_out"][l].astype(BF16), be_out=p["b_exp_out"][l][:, None, :],
        g_mix=p["g_mix"][l][None], g_ffn=p["g_ffn"][l][None],
    )


def _expand_matrix(lane0, heads, inner):
    lanes = jnp.arange(LANES)[:, None]
    cols = jnp.arange(inner)[None, :]
    return (lanes == lane0 + cols // SSD_HEADDIM).astype(BF16)


def _stream(x, mod, lp, dims, cols, nseq, seq_len, seg_len, states, full, g_final=None):
    inner, qk = dims["inner"], dims["qk"]
    h = _norm_call(x, lp["g_mix"], mod[1], mod[0], seq_len, BF16)
    proj = _mm_call(h, lp["w_in"], 512, dims["tn_in"])

    blk = _pick(seq_len, 256)
    tri_lo = _tri_blockdiag(blk, CHUNK, False)
    tri_up = _tri_blockdiag(blk, CHUNK, True)
    heads = qk // GLA_DK
    o_f, gs_f = _gla_call(False, proj, cols, lp["wa_f"], lp["ba_f"], tri_lo, states[0], nseq, seq_len, heads)
    y_gla, gs_b = _gla_call(True, proj, cols, lp["wa_b"], lp["ba_b"], tri_up, states[1], nseq, seq_len,
                            heads, extra=(o_f, lp["gla_g"]))

    xs = _ssd_prep_call(proj, cols["xs"] * LANES // inner, inner, lp["cw_x"], lp["cb_x"], nseq, seq_len)
    wbc = 2 * SSD_GROUPS * SSD_STATE
    bc = _ssd_prep_call(proj, cols["bc"] * LANES // wbc, wbc, lp["cw_bc"], lp["cb_bc"], nseq, seq_len)
    full_lo = _tri_blockdiag(blk, blk, False)
    full_up = _tri_blockdiag(blk, blk, True)
    ssd_heads = inner // SSD_HEADDIM
    ex_f = _expand_matrix(SM_DTF, ssd_heads, inner)
    ex_b = _expand_matrix(SM_DTB, ssd_heads, inner)
    scols = dict(small=cols["small"], z=cols["z"] * LANES // inner)
    y_f, hs_f = _ssd_call(False, xs, bc, proj, scols, lp["dtb"], lp["a_f"], full_lo, ex_f, states[2],
                          nseq, seq_len)
    y_ssd, hs_b = _ssd_call(True, xs, bc, proj, scols, lp["dtb"], lp["a_b"], full_up, ex_b, states[3],
                            nseq, seq_len, extra=(y_f, lp["dskip"], lp["ssd_g"]))
    finals = (gs_f, gs_b, hs_f, hs_b)
    if not full:
        return None, finals

    y_conv = _conv_call(proj, lp["conv_w"], lp["conv_b"], lp["ln_g"], lp["ln_b"], seg_len)
    tn = dims["tn_merge"]
    merged = _merge_call(proj, cols["gate"] * LANES // tn, y_conv, y_gla, y_ssd,
                         lp["w_c"], lp["w_g"], lp["w_s"], tn)
    x1 = _out_call(merged, lp["w_out"], x, mod[2], seq_len)
    h2, top_i, top_w = _router_call(x1, lp["g_ffn"], mod[4], mod[3], lp["wr3"], lp["br"], seq_len)
    ne = lp["we_in"].shape[0]
    tm = MOE_TILE
    plan = _dispatch_plan(top_i[:, :TOP_K], ne, tm)
    y = _experts_call(h2, plan, lp["we_in"], lp["be_in"], lp["we_out"], lp["be_out"], tm)
    x2 = _combine_call(y, top_w, x1, mod[5], seq_len, g_final)
    return x2, finals


def kernel(x, c, ctx, c_ctx, g_mix, g_ffn, w_mod, b_mod, w_in, conv_w, conv_b, conv_ln_g, conv_ln_b,
           w_proj_conv, gla_wa_f, gla_ba_f, gla_wa_b, gla_ba_b, gla_norm_g, w_proj_gla, ssd_conv_w,
           ssd_conv_b, ssd_dt_bias_f, ssd_dt_bias_b, ssd_a_log_f, ssd_a_log_b, ssd_d, ssd_norm_g,
           w_proj_ssd, w_out, w_router, b_router, w_exp_in, b_exp_in, w_exp_out, b_exp_out, g_final):
    p = dict(g_mix=g_mix, g_ffn=g_ffn, w_in=w_in, conv_w=conv_w, conv_b=conv_b, conv_ln_g=conv_ln_g,
             conv_ln_b=conv_ln_b, w_proj_conv=w_proj_conv, gla_wa_f=gla_wa_f, gla_ba_f=gla_ba_f,
             gla_wa_b=gla_wa_b, gla_ba_b=gla_ba_b, gla_norm_g=gla_norm_g, w_proj_gla=w_proj_gla,
             ssd_conv_w=ssd_conv_w, ssd_conv_b=ssd_conv_b, ssd_dt_bias_f=ssd_dt_bias_f,
             ssd_dt_bias_b=ssd_dt_bias_b, ssd_a_log_f=ssd_a_log_f, ssd_a_log_b=ssd_a_log_b,
             ssd_d=ssd_d, ssd_norm_g=ssd_norm_g, w_proj_ssd=w_proj_ssd, w_out=w_out,
             w_router=w_router, b_router=b_router, w_exp_in=w_exp_in, b_exp_in=b_exp_in,
             w_exp_out=w_exp_out, b_exp_out=b_exp_out)
    bsz, t, d = x.shape
    ctx_len = ctx.shape[1]
    depth = w_mod.shape[0]
    ch = conv_w.shape[2]
    qk = gla_wa_f.shape[2]
    ssd_heads = ssd_d.shape[1]
    inner = ssd_heads * SSD_HEADDIM
    assert ssd_heads <= 16 and bsz + 1 <= 8
    widths = [("glu", 2 * ch), ("q", qk), ("k", qk), ("v", qk), ("r", qk), ("z", inner), ("xs", inner),
              ("bc", 2 * SSD_GROUPS * SSD_STATE), ("gate", N_BRANCH * d), ("small", LANES)]
    cols, o = {}, 0
    for name, wdt in widths:
        assert o % LANES == 0
        cols[name] = o // LANES
        o += wdt
    tn_merge = math.gcd(cols["gate"] * LANES, _pick(d, 1024, LANES))
    dims = dict(d=d, ch=ch, qk=qk, inner=inner, ssd_heads=ssd_heads,
                tn_in=min(1536, o), tn_merge=tn_merge)
    assert (cols["z"] * LANES) % inner == 0 and (cols["xs"] * LANES) % inner == 0
    assert (cols["bc"] * LANES) % (2 * SSD_GROUPS * SSD_STATE) == 0
    assert (cols["gate"] * LANES) % tn_merge == 0 and (cols["glu"] == 0)

    cin = jnp.zeros((8, d), F32).at[:bsz].set(c).at[bsz].set(c_ctx)
    mods = _mod_call(cin, w_mod, b_mod[:, None, :])

    xl = x.reshape(bsz * t, d)
    xc = ctx.reshape(bsz * ctx_len, d)
    heads = qk // GLA_DK
    zero_g = jnp.zeros((bsz, heads, GLA_DV, GLA_DK), F32)
    zero_s = jnp.zeros((bsz, SSD_GROUPS, inner // SSD_GROUPS, SSD_STATE), F32)
    zeros = (zero_g, zero_g, zero_s, zero_s)
    for l in range(depth):
        lp = _layer_params(l, p, dims)
        m = mods[l].reshape(8, N_MOD, d)
        mod_l = jnp.transpose(m[:bsz], (1, 0, 2))[:, :, None, :]
        mod_c = m[bsz][:, None, None, :]
        last = l == depth - 1
        xc_new, ctx_states = _stream(xc, mod_c, lp, dims, cols, bsz, ctx_len, ctx_len, zeros,
                                     full=not last)
        xl, _ = _stream(xl, mod_l, lp, dims, cols, bsz, t, GRID_W, ctx_states, full=True,
                        g_final=g_final[None] if last else None)
        if not last:
            xc = xc_new
    return xl.reshape(bsz, t, d)
```

```python
import functools
import math

import jax
import jax.numpy as jnp
from jax import lax
from jax.experimental import pallas as pl
from jax.experimental.pallas import tpu as pltpu

F32 = jnp.float32
BF16 = jnp.bfloat16

GRID_W = 64
N_MOD = 6
N_BRANCH = 3
EPS = 1e-6
CHUNK = 64
GLA_DK = 128
GLA_DV = 128
GLA_RANK = 16
GLA_TAU = 16.0
SSD_HEADDIM = 64
SSD_GROUPS = 2
SSD_STATE = 128
TOP_K = 4
SWIGLU_LIMIT = 7.0
SWIGLU_ALPHA = 1.702

LANES = 128
VMEM_LIMIT = 56 * 1024 * 1024
NEG_BIG = -1e30
MOE_TILE = 256

SM_AF, SM_AB, SM_DTF, SM_DTB = 0, 16, 32, 48


def _cparams(sem):
    return pltpu.CompilerParams(dimension_semantics=sem, vmem_limit_bytes=VMEM_LIMIT)


def _pick(n, pref, mult=8):
    if n <= pref:
        return n
    t = (pref // mult) * mult
    while t >= mult:
        if n % t == 0:
            return t
        t -= mult
    return n


def _dot(a, b):
    return jnp.dot(a.astype(BF16), b.astype(BF16), preferred_element_type=F32)


def _dot_nt(a, b):
    return lax.dot_general(a.astype(BF16), b.astype(BF16), (((1,), (1,)), ((), ())),
                           preferred_element_type=F32)


def _dot_tn(a, b):
    return lax.dot_general(a.astype(BF16), b.astype(BF16), (((0,), (0,)), ((), ())),
                           preferred_element_type=F32)


def _split2(x):
    hi = x.astype(BF16)
    lo = (x - hi.astype(F32)).astype(BF16)
    return hi, lo


def _split3(x):
    hi = x.astype(BF16)
    r = x - hi.astype(F32)
    mid = r.astype(BF16)
    lo = (r - mid.astype(F32)).astype(BF16)
    return hi, mid, lo


def _sel_dot(sel, x):
    hi, mid, lo = _split3(x)
    dot = lambda p: jnp.dot(sel, p, preferred_element_type=F32)
    return dot(hi) + dot(mid) + dot(lo)


def _dot_sel(x, sel):
    hi, lo = _split2(x)
    return (jnp.dot(hi, sel, preferred_element_type=F32)
            + jnp.dot(lo, sel, preferred_element_type=F32))


def _sigmoid(x):
    return 1.0 / (1.0 + jnp.exp(-x))


def _silu(x):
    return x * _sigmoid(x)


def _softplus(x):
    return jnp.maximum(x, 0.0) + jnp.log(1.0 + jnp.exp(-jnp.abs(x)))


def _log_sigmoid(x):
    return jnp.minimum(x, 0.0) - jnp.log(1.0 + jnp.exp(-jnp.abs(x)))


def _mod_kernel(nvec, c_ref, w_ref, b_ref, o_ref, act_ref):
    @pl.when(jnp.logical_and(pl.program_id(0) == 0, pl.program_id(1) == 0))
    def _():
        act_ref[...] = _silu(c_ref[...])

    d, tn = w_ref.shape
    nj = tn // LANES

    def body(kb, accs):
        rows8 = pl.ds(pl.multiple_of(kb * 8, 8), 8)
        w_blk = w_ref[rows8, :]
        new = []
        for v in range(nvec):
            a_blk = act_ref[v, rows8, :]
            new.append(tuple(accs[v][j] + w_blk[:, j * LANES:(j + 1) * LANES] * a_blk
                             for j in range(nj)))
        return tuple(new)

    zero = jnp.zeros((8, LANES), F32)
    accs = lax.fori_loop(0, d // 8, body, tuple((zero,) * nj for _ in range(nvec)), unroll=8)
    rows = [jnp.concatenate([jnp.sum(a, axis=0, keepdims=True) for a in accs[v]], axis=1)
            for v in range(nvec)]
    rows.append(jnp.zeros((8 - nvec, tn), F32))
    o_ref[...] = jnp.concatenate(rows, axis=0) + b_ref[...]


def _mod_call(cvecs, w_mod, b_mod):
    nvec, d = cvecs.shape
    nl, _, n = w_mod.shape
    tn = _pick(n, 512, LANES)
    crep = jnp.broadcast_to(cvecs[:, :, None], (nvec, d, LANES))
    return pl.pallas_call(
        functools.partial(_mod_kernel, nvec),
        grid=(nl, n // tn),
        in_specs=[pl.BlockSpec((nvec, d, LANES), lambda l, j: (0, 0, 0)),
                  pl.BlockSpec((None, d, tn), lambda l, j: (l, 0, j)),
                  pl.BlockSpec((None, 1, tn), lambda l, j: (l, 0, j))],
        out_specs=pl.BlockSpec((None, 8, tn), lambda l, j: (l, 0, j)),
        out_shape=jax.ShapeDtypeStruct((nl, 8, n), F32),
        scratch_shapes=[pltpu.VMEM((nvec, d, LANES), F32)],
        compiler_params=_cparams(("arbitrary", "arbitrary")),
        name="mod_vectors",
    )(crep, w_mod, b_mod)


def _norm_mod(x, g, sc, sh):
    y = x * lax.rsqrt(jnp.mean(x * x, axis=-1, keepdims=True) + EPS) * g
    return y * (1.0 + sc) + sh


def _norm_kernel(x_ref, g_ref, sc_ref, sh_ref, o_ref):
    o_ref[...] = _norm_mod(x_ref[...], g_ref[...], sc_ref[...], sh_ref[...]).astype(o_ref.dtype)


def _norm_call(x, g, sc, sh, rows_per_group, out_dtype):
    r, d = x.shape
    tm = _pick(rows_per_group, 256)
    per = rows_per_group // tm
    ng = sc.shape[0]
    gidx = (lambda i: (i // per, 0, 0)) if ng > 1 else (lambda i: (0, 0, 0))
    return pl.pallas_call(
        _norm_kernel,
        grid=(r // tm,),
        in_specs=[pl.BlockSpec((tm, d), lambda i: (i, 0)),
                  pl.BlockSpec((1, d), lambda i: (0, 0)),
                  pl.BlockSpec((None, 1, d), gidx),
                  pl.BlockSpec((None, 1, d), gidx)],
        out_specs=pl.BlockSpec((tm, d), lambda i: (i, 0)),
        out_shape=jax.ShapeDtypeStruct((r, d), out_dtype),
        compiler_params=_cparams(("arbitrary",)),
        name="norm_mod",
    )(x, g, sc, sh)


def _mm_kernel(a_ref, b_ref, o_ref):
    o_ref[...] = jnp.dot(a_ref[...], b_ref[...], preferred_element_type=F32)


def _mm_call(a, b, tm_pref, tn):
    r, k = a.shape
    n = b.shape[1]
    tm = _pick(r, tm_pref)
    assert n % tn == 0
    return pl.pallas_call(
        _mm_kernel,
        grid=(n // tn, r // tm),
        in_specs=[pl.BlockSpec((tm, k), lambda j, i: (i, 0)),
                  pl.BlockSpec((k, tn), lambda j, i: (0, j))],
        out_specs=pl.BlockSpec((tm, tn), lambda j, i: (i, j)),
        out_shape=jax.ShapeDtypeStruct((r, n), F32),
        compiler_params=_cparams(("arbitrary", "arbitrary")),
        name="in_proj",
    )(a, b)


def _conv_kernel(seg_len, glu_ref, w_ref, b_ref, lg_ref, lb_ref, o_ref):
    blk, ch = o_ref.shape
    kw = w_ref.shape[0]
    half = kw // 2
    u = glu_ref[:, :ch] * _sigmoid(glu_ref[:, ch:])
    pos = lax.broadcasted_iota(jnp.int32, (blk, 1), 0) % seg_len
    acc = jnp.zeros((blk, ch), F32)
    for j in range(kw):
        d = j - half
        shifted = u if d == 0 else pltpu.roll(u, (-d) % blk, 0)
        valid = jnp.logical_and(pos + d >= 0, pos + d < seg_len)
        acc = acc + jnp.where(valid, shifted, 0.0) * w_ref[j:j + 1, :]
    y = acc + b_ref[...]
    mu = jnp.mean(y, axis=-1, keepdims=True)
    yc = y - mu
    var = jnp.mean(yc * yc, axis=-1, keepdims=True)
    z = yc * lax.rsqrt(var + EPS) * lg_ref[...] + lb_ref[...]
    o_ref[...] = _silu(z).astype(o_ref.dtype)


def _conv_call(proj, conv_w, conv_b, ln_g, ln_b, seg_len):
    r = proj.shape[0]
    kw, ch = conv_w.shape
    blk = seg_len * max(1, 256 // seg_len)
    assert r % blk == 0
    vec = lambda: pl.BlockSpec((1, ch), lambda i: (0, 0))
    return pl.pallas_call(
        functools.partial(_conv_kernel, seg_len),
        grid=(r // blk,),
        in_specs=[pl.BlockSpec((blk, 2 * ch), lambda i: (i, 0)),
                  pl.BlockSpec((kw, ch), lambda i: (0, 0)),
                  vec(), vec(), vec()],
        out_specs=pl.BlockSpec((blk, ch), lambda i: (i, 0)),
        out_shape=jax.ShapeDtypeStruct((r, ch), BF16),
        compiler_params=_cparams(("arbitrary",)),
        name="conformer_conv",
    )(proj, conv_w, conv_b, ln_g, ln_b)


def _ssd_prep_kernel(nblk, cur_ref, prev_ref, next_ref, w_ref, b_ref, o_ref):
    blk, ch = o_ref.shape
    kw = w_ref.shape[0]
    half = kw // 2
    i = pl.program_id(1)
    prev = jnp.where(i > 0, prev_ref[...], 0.0)
    nxt = jnp.where(i < nblk - 1, next_ref[...], 0.0)
    ext = jnp.concatenate([prev, cur_ref[...], nxt], axis=0)
    n = blk + 16
    acc = jnp.zeros((blk, ch), F32)
    for j in range(kw):
        d = j - half
        shifted = ext if d == 0 else pltpu.roll(ext, (-d) % n, 0)
        acc = acc + shifted[8:8 + blk, :] * w_ref[j:j + 1, :]
    o_ref[...] = _silu(acc + b_ref[...])


def _ssd_prep_call(proj, col_blk, width, w, b, nseq, seq_len):
    r = proj.shape[0]
    kw = w.shape[0]
    blk = _pick(seq_len, 256)
    nblk = seq_len // blk
    b8 = blk // 8
    nb8 = r // 8
    cur = lambda s, i: (s * nblk + i, col_blk)
    prev = lambda s, i: (jnp.maximum((s * nblk + i) * b8 - 1, 0), col_blk)
    nxt = lambda s, i: (jnp.minimum((s * nblk + i + 1) * b8, nb8 - 1), col_blk)
    return pl.pallas_call(
        functools.partial(_ssd_prep_kernel, nblk),
        grid=(nseq, nblk),
        in_specs=[pl.BlockSpec((blk, width), cur),
                  pl.BlockSpec((8, width), prev),
                  pl.BlockSpec((8, width), nxt),
                  pl.BlockSpec((kw, width), lambda s, i: (0, 0)),
                  pl.BlockSpec((1, width), lambda s, i: (0, 0))],
        out_specs=pl.BlockSpec((blk, width), lambda s, i: (s * nblk + i, 0)),
        out_shape=jax.ShapeDtypeStruct((r, width), F32),
        compiler_params=_cparams(("arbitrary", "arbitrary")),
        name="ssd_conv",
    )(proj, proj, proj, w, b)


def _gla_kernel(rev, last, nblk, *refs):
    if last:
        (q_ref, k_ref, v_ref, sm_ref, wa_ref, ba_ref, tri_ref, s0_ref, of_ref, r_ref, g_ref,
         o_ref, sfin_ref, st_ref, sp_ref) = refs
    else:
        (q_ref, k_ref, v_ref, sm_ref, wa_ref, ba_ref, tri_ref, s0_ref,
         o_ref, sfin_ref, st_ref, sp_ref) = refs
    i = pl.program_id(2)

    @pl.when(i == 0)
    def _():
        st_ref[...] = s0_ref[...]

    blk = q_ref.shape[0]
    hps = q_ref.shape[1] // GLA_DK
    nch = blk // CHUNK
    ref_i = CHUNK // 2 - 1 if rev else CHUNK // 2
    last_i = 0 if rev else CHUNK - 1
    bcast = lambda t: jnp.broadcast_to(t, (nch, CHUNK, GLA_DK)).reshape(blk, GLA_DK)
    row = lax.broadcasted_iota(jnp.int32, (blk, blk), 0)
    col = lax.broadcasted_iota(jnp.int32, (blk, blk), 1)
    same = (row // CHUNK) == (col // CHUNK)
    tri = (row <= col) if rev else (row >= col)
    keep = jnp.logical_and(same, tri)
    sel = tri_ref[...]
    sm = sm_ref[...].astype(BF16)
    order = range(nch - 1, -1, -1) if rev else range(nch)

    for hh in range(hps):
        hs = slice(hh * GLA_DK, (hh + 1) * GLA_DK)
        z = jnp.dot(sm, wa_ref[:, hs], preferred_element_type=F32) + ba_ref[:, hs]
        lg = _log_sigmoid(z) * (1.0 / GLA_TAU)
        g = _sel_dot(sel, lg)
        g3 = g.reshape(nch, CHUNK, GLA_DK)
        g_ref_pt = bcast(g3[:, ref_i:ref_i + 1, :])
        g_last3 = g3[:, last_i:last_i + 1, :]
        g_last = bcast(g_last3)

        q = q_ref[:, hs] * (GLA_DK ** -0.5)
        k = k_ref[:, hs]
        v = v_ref[:, hs].astype(BF16)
        qe = q * jnp.exp(g - g_ref_pt)
        ke = k * jnp.exp(g_ref_pt - g)
        att = jnp.where(keep, _dot_nt(qe, ke), 0.0)
        y = _dot(att, v)

        kd = (k * jnp.exp(g_last - g)).astype(BF16)
        qg = (q * jnp.exp(g)).astype(BF16)
        dec = jnp.exp(g_last3)
        st = st_ref[hh]
        for c in order:
            sl = slice(c * CHUNK, (c + 1) * CHUNK)
            sp_ref[hh, c] = st
            st = dec[c] * st + _dot_tn(v[sl, :], kd[sl, :])
        st_ref[hh] = st
        y_off = [_dot_nt(qg[c * CHUNK:(c + 1) * CHUNK, :], sp_ref[hh, c]) for c in range(nch)]
        y = y + jnp.concatenate(y_off, axis=0)

        if last:
            o = of_ref[:, hs] + y
            o = o * lax.rsqrt(jnp.mean(o * o, axis=-1, keepdims=True) + EPS) * g_ref[:, hs]
            o_ref[:, hs] = (o * _silu(r_ref[:, hs])).astype(o_ref.dtype)
        else:
            o_ref[:, hs] = y

    @pl.when(i == nblk - 1)
    def _():
        sfin_ref[...] = st_ref[...]


def _gla_call(rev, proj, cols, wa_pad, ba, tri, s0, nseq, seq_len, heads, extra=None):
    r = proj.shape[0]
    blk = tri.shape[0]
    nblk = seq_len // blk
    last = extra is not None
    hps = next(n for n in (4, 2, 1)
               if heads % n == 0 and all(cols[c] % n == 0 for c in ("q", "k", "v", "r")))
    wid = hps * LANES
    tok = (lambda s, i: s * nblk + (nblk - 1 - i)) if rev else (lambda s, i: s * nblk + i)
    colspec = lambda off: pl.BlockSpec((blk, wid), lambda s, h, i: (tok(s, i), off // hps + h))
    hvec = pl.BlockSpec((1, wid), lambda s, h, i: (0, h))
    state = pl.BlockSpec((None, hps, GLA_DV, GLA_DK), lambda s, h, i: (s, h, 0, 0))
    in_specs = [colspec(cols["q"]), colspec(cols["k"]), colspec(cols["v"]),
                pl.BlockSpec((blk, LANES), lambda s, h, i: (tok(s, i), cols["small"])),
                pl.BlockSpec((LANES, wid), lambda s, h, i: (0, h)),
                hvec,
                pl.BlockSpec((blk, blk), lambda s, h, i: (0, 0)),
                state]
    args = [proj, proj, proj, proj, wa_pad, ba, tri, s0]
    if last:
        o_f, norm_g = extra
        in_specs += [pl.BlockSpec((blk, wid), lambda s, h, i: (tok(s, i), h)),
                     colspec(cols["r"]), hvec]
        args += [o_f, proj, norm_g]
    out_dtype = BF16 if last else F32
    return pl.pallas_call(
        functools.partial(_gla_kernel, rev, last, nblk),
        grid=(nseq, heads // hps, nblk),
        in_specs=in_specs,
        out_specs=[pl.BlockSpec((blk, wid), lambda s, h, i: (tok(s, i), h)), state],
        out_shape=[jax.ShapeDtypeStruct((r, heads * GLA_DV), out_dtype),
                   jax.ShapeDtypeStruct((nseq, heads, GLA_DV, GLA_DK), F32)],
        scratch_shapes=[pltpu.VMEM((hps, GLA_DV, GLA_DK), F32),
                        pltpu.VMEM((hps, blk // CHUNK, GLA_DV, GLA_DK), F32)],
        compiler_params=_cparams(("arbitrary", "arbitrary", "arbitrary")),
        name="gla_bwd" if rev else "gla_fwd",
    )(*args)


def _ssd_kernel(rev, last, nblk, lane0, *refs):
    if last:
        (xs_ref, bc_ref, sm_ref, dtb_ref, a_ref, tri_ref, ex_ref, h0_ref, yf_ref, z_ref, d_ref,
         ng_ref, o_ref, hfin_ref, h_ref) = refs
    else:
        (xs_ref, bc_ref, sm_ref, dtb_ref, a_ref, tri_ref, ex_ref, h0_ref,
         o_ref, hfin_ref, h_ref) = refs
    i = pl.program_id(1)

    @pl.when(i == 0)
    def _():
        h_ref[...] = h0_ref[...]

    blk, inner = xs_ref.shape
    gsz = inner // SSD_GROUPS
    epg = gsz // SSD_HEADDIM
    ns = SSD_STATE

    dt = _softplus(sm_ref[...] + dtb_ref[...])
    ad = a_ref[...] * dt
    cum = _sel_dot(tri_ref[...], ad)
    tot = cum[0:1, :] if rev else cum[blk - 1:blk, :]
    cum_t = cum.T
    ex = ex_ref[...]
    xs = xs_ref[...]
    xd = (xs * _dot_sel(dt, ex)).astype(BF16)
    xdw = (xs * _dot_sel(dt * jnp.exp(tot - cum), ex)).astype(BF16)
    ecum = _dot_sel(jnp.exp(cum), ex)

    row = lax.broadcasted_iota(jnp.int32, (blk, blk), 0)
    col = lax.broadcasted_iota(jnp.int32, (blk, blk), 1)
    causal = (row <= col) if rev else (row >= col)
    lane = lax.broadcasted_iota(jnp.int32, (blk, LANES), 1)
    low = lane < SSD_HEADDIM

    y_cols = []
    for g in range(SSD_GROUPS):
        bm = bc_ref[:, g * ns:(g + 1) * ns].astype(BF16)
        cm = bc_ref[:, (SSD_GROUPS + g) * ns:(SSD_GROUPS + g + 1) * ns].astype(BF16)
        cb = _dot_nt(cm, bm)
        hg = h_ref[g]
        y_off = _dot_nt(cm, hg)
        st_new = _dot_tn(xdw[:, g * gsz:(g + 1) * gsz], bm)
        for pr in range(epg // 2):
            c0 = g * gsz + pr * LANES
            xpair = xd[:, c0:c0 + LANES]
            acc = y_off[:, pr * LANES:(pr + 1) * LANES] * ecum[:, c0:c0 + LANES]
            for half in range(2):
                ln = lane0 + g * epg + 2 * pr + half
                seg = cum[:, ln:ln + 1] - cum_t[ln:ln + 1, :]
                w = (cb * jnp.exp(jnp.where(causal, seg, NEG_BIG))).astype(BF16)
                keep = low if half == 0 else jnp.logical_not(low)
                rhs = jnp.where(keep, xpair, jnp.zeros_like(xpair))
                acc = acc + jnp.dot(w, rhs, preferred_element_type=F32)
            y_cols.append(acc)
        for e in range(epg):
            ln = lane0 + g * epg + e
            rs = slice(e * SSD_HEADDIM, (e + 1) * SSD_HEADDIM)
            dec = jnp.broadcast_to(jnp.exp(tot[:, ln:ln + 1]), (SSD_HEADDIM, ns))
            h_ref[g, rs, :] = dec * hg[rs, :] + st_new[rs, :]
    y = jnp.concatenate(y_cols, axis=1)

    @pl.when(i == nblk - 1)
    def _():
        hfin_ref[...] = h_ref[...]

    if last:
        ys = (yf_ref[...] + y + d_ref[...] * xs) * _silu(z_ref[...])
        outs = []
        for g in range(SSD_GROUPS):
            yg = ys[:, g * gsz:(g + 1) * gsz]
            outs.append(yg * lax.rsqrt(jnp.mean(yg * yg, axis=-1, keepdims=True) + EPS))
        o_ref[...] = (jnp.concatenate(outs, axis=1) * ng_ref[...]).astype(o_ref.dtype)
    else:
        o_ref[...] = y


def _ssd_call(rev, xs, bc, proj, cols, dtb, a_row, tri, ex, h0, nseq, seq_len, extra=None):
    r, inner = xs.shape
    blk = tri.shape[0]
    nblk = seq_len // blk
    last = extra is not None
    gsz = inner // SSD_GROUPS
    lane0 = SM_DTB if rev else SM_DTF
    tok = (lambda s, i: s * nblk + (nblk - 1 - i)) if rev else (lambda s, i: s * nblk + i)
    rowblk = lambda w: pl.BlockSpec((blk, w), lambda s, i: (tok(s, i), 0))
    const = lambda shp: pl.BlockSpec(shp, lambda s, i: (0,) * len(shp))
    state = pl.BlockSpec((None, SSD_GROUPS, gsz, SSD_STATE), lambda s, i: (s, 0, 0, 0))
    in_specs = [rowblk(inner), rowblk(bc.shape[1]),
                pl.BlockSpec((blk, LANES), lambda s, i: (tok(s, i), cols["small"])),
                const((1, LANES)), const((1, LANES)), const((blk, blk)), const((LANES, inner)),
                state]
    args = [xs, bc, proj, dtb, a_row, tri, ex, h0]
    if last:
        y_f, dskip, norm_g = extra
        in_specs += [rowblk(inner),
                     pl.BlockSpec((blk, inner), lambda s, i: (tok(s, i), cols["z"])),
                     const((1, inner)), const((1, inner))]
        args += [y_f, proj, dskip, norm_g]
    return pl.pallas_call(
        functools.partial(_ssd_kernel, rev, last, nblk, lane0),
        grid=(nseq, nblk),
        in_specs=in_specs,
        out_specs=[rowblk(inner), state],
        out_shape=[jax.ShapeDtypeStruct((r, inner), BF16 if last else F32),
                   jax.ShapeDtypeStruct((nseq, SSD_GROUPS, gsz, SSD_STATE), F32)],
        scratch_shapes=[pltpu.VMEM((SSD_GROUPS, gsz, SSD_STATE), F32)],
        compiler_params=_cparams(("arbitrary", "arbitrary")),
        name="ssd_bwd" if rev else "ssd_fwd",
    )(*args)


def _merge_kernel(gc_ref, gg_ref, gs_ref, yc_ref, yg_ref, ys_ref, wc_ref, wg_ref, ws_ref, o_ref):
    dot = lambda a, b: jnp.dot(a[...], b[...], preferred_element_type=F32)
    acc = _sigmoid(gc_ref[...]) * dot(yc_ref, wc_ref)
    acc = acc + _sigmoid(gg_ref[...]) * dot(yg_ref, wg_ref)
    acc = acc + _sigmoid(gs_ref[...]) * dot(ys_ref, ws_ref)
    o_ref[...] = acc.astype(o_ref.dtype)


def _merge_call(proj, gate_col, y_conv, y_gla, y_ssd, w_c, w_g, w_s, tn):
    r = proj.shape[0]
    d = w_c.shape[1]
    tm = _pick(r, 512)
    nj = d // tn
    gate = lambda b: pl.BlockSpec((tm, tn), lambda j, i: (i, gate_col + b * nj + j))
    yspec = lambda y: pl.BlockSpec((tm, y.shape[1]), lambda j, i: (i, 0))
    wspec = lambda w: pl.BlockSpec((w.shape[0], tn), lambda j, i: (0, j))
    return pl.pallas_call(
        _merge_kernel,
        grid=(nj, r // tm),
        in_specs=[gate(0), gate(1), gate(2), yspec(y_conv), yspec(y_gla), yspec(y_ssd),
                  wspec(w_c), wspec(w_g), wspec(w_s)],
        out_specs=pl.BlockSpec((tm, tn), lambda j, i: (i, j)),
        out_shape=jax.ShapeDtypeStruct((r, d), BF16),
        compiler_params=_cparams(("arbitrary", "arbitrary")),
        name="gated_merge",
    )(proj, proj, proj, y_conv, y_gla, y_ssd, w_c, w_g, w_s)


def _out_kernel(m_ref, w_ref, x_ref, gt_ref, o_ref):
    o_ref[...] = x_ref[...] + gt_ref[...] * jnp.dot(m_ref[...], w_ref[...],
                                                     preferred_element_type=F32)


def _out_call(merged, w_out, x, gate, rows_per_group):
    r, d = x.shape
    tm = _pick(rows_per_group, 512)
    tn = _pick(d, 1024, LANES)
    per = rows_per_group // tm
    gidx = (lambda j, i: (i // per, 0, j)) if gate.shape[0] > 1 else (lambda j, i: (0, 0, j))
    return pl.pallas_call(
        _out_kernel,
        grid=(d // tn, r // tm),
        in_specs=[pl.BlockSpec((tm, d), lambda j, i: (i, 0)),
                  pl.BlockSpec((d, tn), lambda j, i: (0, j)),
                  pl.BlockSpec((tm, tn), lambda j, i: (i, j)),
                  pl.BlockSpec((None, 1, tn), gidx)],
        out_specs=pl.BlockSpec((tm, tn), lambda j, i: (i, j)),
        out_shape=jax.ShapeDtypeStruct((r, d), F32),
        compiler_params=_cparams(("arbitrary", "arbitrary")),
        name="out_proj",
    )(merged, w_out, x, gate)


def _router_kernel(x_ref, g_ref, sc_ref, sh_ref, wr_ref, br_ref, h_ref, idx_ref, w_ref):
    h = _norm_mod(x_ref[...], g_ref[...], sc_ref[...], sh_ref[...])
    hhi, hlo = _split2(h)
    whi, wmid, wlo = wr_ref[0], wr_ref[1], wr_ref[2]
    dot = lambda a, b: jnp.dot(a, b, preferred_element_type=F32)
    logits = (dot(hhi, whi) + dot(hhi, wmid) + dot(hlo, whi) + dot(hhi, wlo) + dot(hlo, wmid)
              + br_ref[...])
    tm, ne = logits.shape
    lane = lax.broadcasted_iota(jnp.int32, (tm, ne), 1).astype(F32)
    work = logits
    idxs, vals = [], []
    for _ in range(TOP_K):
        m = jnp.max(work, axis=-1, keepdims=True)
        idx = jnp.min(jnp.where(work == m, lane, float(ne)), axis=-1, keepdims=True)
        idxs.append(idx)
        vals.append(m)
        work = jnp.where(lane == idx, -jnp.inf, work)
    es = [jnp.exp(v - vals[0]) for v in vals]
    inv = 1.0 / sum(es)
    out_lane = lax.broadcasted_iota(jnp.int32, (tm, LANES), 1)
    idx_out = jnp.zeros((tm, LANES), F32)
    w_out = jnp.zeros((tm, LANES), F32)
    for k in range(TOP_K):
        idx_out = jnp.where(out_lane == k, idxs[k], idx_out)
        w_out = jnp.where(out_lane == k, es[k] * inv, w_out)
    idx_ref[...] = idx_out.astype(jnp.int32)
    w_ref[...] = w_out
    h_ref[...] = _pack_halves(h)


def _pack_halves(x):
    half = x.shape[1] // 2
    bits = lambda t: lax.bitcast_convert_type(t.astype(BF16).astype(F32), jnp.uint32)
    return (bits(x[:, half:]) & jnp.uint32(0xFFFF0000)) | (bits(x[:, :half]) >> 16)


def _unpack_halves(w):
    lo = lax.bitcast_convert_type(w << 16, F32)
    hi = lax.bitcast_convert_type(w & jnp.uint32(0xFFFF0000), F32)
    return lo, hi


def _router_call(x, g, sc, sh, wr3, br, rows_per_group):
    r, d = x.shape
    ne = wr3.shape[2]
    tm = _pick(rows_per_group, 256)
    per = rows_per_group // tm
    gidx = (lambda i: (i // per, 0, 0)) if sc.shape[0] > 1 else (lambda i: (0, 0, 0))
    row = lambda w: pl.BlockSpec((tm, w), lambda i: (i, 0))
    return pl.pallas_call(
        _router_kernel,
        grid=(r // tm,),
        in_specs=[row(d),
                  pl.BlockSpec((1, d), lambda i: (0, 0)),
                  pl.BlockSpec((None, 1, d), gidx),
                  pl.BlockSpec((None, 1, d), gidx),
                  pl.BlockSpec((3, d, ne), lambda i: (0, 0, 0)),
                  pl.BlockSpec((1, ne), lambda i: (0, 0))],
        out_specs=[row(d // 2), row(LANES), row(LANES)],
        out_shape=[jax.ShapeDtypeStruct((r, d // 2), jnp.uint32),
                   jax.ShapeDtypeStruct((r, LANES), jnp.int32),
                   jax.ShapeDtypeStruct((r, LANES), F32)],
        compiler_params=_cparams(("arbitrary",)),
        name="router",
    )(x, g, sc, sh, wr3, br)


def _dispatch_plan(top_i, ne, tm):
    n, topk = top_i.shape
    n_tiles = (n * topk) // tm + ne
    onehot = (top_i[:, :, None] == jnp.arange(ne, dtype=jnp.int32)[None, None, :]).any(axis=1)
    onehot = onehot.astype(jnp.int32)
    cnt = onehot.sum(axis=0)
    rank = jnp.cumsum(onehot, axis=0) - onehot
    tiles_e = (cnt + tm - 1) // tm
    tile_end = jnp.cumsum(tiles_e)
    tile_start = tile_end - tiles_e
    n_valid = tile_end[-1]
    slot = tile_start[top_i] * tm + jnp.take_along_axis(rank, top_i, axis=1)
    tok = jnp.arange(n, dtype=jnp.int32)[:, None]
    dst = tok + jnp.arange(topk, dtype=jnp.int32)[None, :] * n
    dst_of_slot = jnp.zeros((n_tiles * tm,), jnp.int32).at[slot.reshape(-1)].set(dst.reshape(-1))
    tok_of_slot = dst_of_slot % n
    t = jnp.arange(n_tiles, dtype=jnp.int32)
    te = jnp.sum((tile_end[None, :] <= t[:, None]).astype(jnp.int32), axis=1)
    te = jnp.minimum(te, ne - 1)
    n_real = jnp.clip(cnt[te] - (t - tile_start[te]) * tm, 0, tm).astype(jnp.int32)
    return (te, n_real, n_valid.astype(jnp.int32).reshape(1),
            tok_of_slot.reshape(n_tiles, 1, tm), dst_of_slot.reshape(n_tiles, 1, tm))


def _experts_kernel(tm, te_ref, nr_ref, nv_ref, tok_ref, tokn_ref, dst_ref, h_hbm, wi_ref, bi_ref,
                    wo_ref, bo_ref, y_hbm, gbuf, ybuf, gsem, ssem):
    t = pl.program_id(0)
    nv = nv_ref[0]
    slot = t % 2

    def row_gather(ids_ref, s, r):
        return pltpu.make_async_copy(h_hbm.at[pl.ds(ids_ref[0, r], 1)], gbuf.at[s, pl.ds(r, 1)],
                                     gsem.at[s])

    def row_scatter(s, r, dst_row):
        return pltpu.make_async_copy(ybuf.at[s, pl.ds(r, 1)], y_hbm.at[pl.ds(dst_row, 1)],
                                     ssem.at[s])

    def wait_scatter(s, n):
        @pl.when(n == tm)
        def _():
            pltpu.make_async_copy(ybuf.at[s], y_hbm.at[pl.ds(0, tm)], ssem.at[s]).wait()

        @pl.when(n != tm)
        def _():
            def body(r, c):
                row_scatter(s, 0, 0).wait()
                return c
            lax.fori_loop(0, n, body, 0)

    @pl.when(t == 0)
    def _():
        def body(r, c):
            row_gather(tok_ref, 0, r).start()
            return c
        lax.fori_loop(0, tm, body, 0, unroll=8)

    @pl.when(t + 1 < nv)
    def _():
        for r in range(tm):
            row_gather(tokn_ref, 1 - slot, r).start()

    @pl.when(t < nv)
    def _():
        pltpu.make_async_copy(h_hbm.at[pl.ds(0, tm)], gbuf.at[slot], gsem.at[slot]).wait()

        @pl.when(t >= 2)
        def _():
            wait_scatter(slot, nr_ref[t - 2])

        ff = wo_ref.shape[0]
        half = wi_ref.shape[0] // 2
        lo, hi = _unpack_halves(gbuf[slot])
        gu = (jnp.dot(lo.astype(BF16), wi_ref[:half, :], preferred_element_type=F32)
              + jnp.dot(hi.astype(BF16), wi_ref[half:, :], preferred_element_type=F32) + bi_ref[...])
        gate = jnp.minimum(gu[:, :ff], SWIGLU_LIMIT)
        up = jnp.clip(gu[:, ff:], -SWIGLU_LIMIT, SWIGLU_LIMIT)
        act = gate * _sigmoid(SWIGLU_ALPHA * gate) * (up + 1.0)
        y = jnp.dot(act.astype(BF16), wo_ref[...], preferred_element_type=F32) + bo_ref[...]
        ybuf[slot] = _pack_halves(y)

        n_real = nr_ref[t]

        @pl.when(n_real == tm)
        def _():
            for r in range(tm):
                row_scatter(slot, r, dst_ref[0, r]).start()

        @pl.when(n_real != tm)
        def _():
            def body(r, c):
                row_scatter(slot, r, dst_ref[0, r]).start()
                return c
            lax.fori_loop(0, n_real, body, 0)

        @pl.when(t == nv - 1)
        def _():
            wait_scatter(slot, n_real)

            @pl.when(t >= 1)
            def _():
                wait_scatter(1 - slot, nr_ref[t - 1])


def _experts_call(h, plan, w_in, b_in, w_out, b_out, tm):
    n, dh = h.shape
    d = 2 * dh
    te, n_real, n_valid, tok, dst = plan
    n_tiles = te.shape[0]
    ne, _, ff2 = w_in.shape
    ff = ff2 // 2
    ids = lambda f: pl.BlockSpec((None, 1, tm), f, memory_space=pltpu.SMEM)
    grid_spec = pltpu.PrefetchScalarGridSpec(
        num_scalar_prefetch=3,
        grid=(n_tiles,),
        in_specs=[ids(lambda t, te, nr, nv: (t, 0, 0)),
                  ids(lambda t, te, nr, nv: (jnp.minimum(t + 1, n_tiles - 1), 0, 0)),
                  ids(lambda t, te, nr, nv: (t, 0, 0)),
                  pl.BlockSpec(memory_space=pl.ANY),
                  pl.BlockSpec((None, d, ff2), lambda t, te, nr, nv: (te[t], 0, 0)),
                  pl.BlockSpec((None, 1, ff2), lambda t, te, nr, nv: (te[t], 0, 0)),
                  pl.BlockSpec((None, ff, d), lambda t, te, nr, nv: (te[t], 0, 0)),
                  pl.BlockSpec((None, 1, d), lambda t, te, nr, nv: (te[t], 0, 0))],
        out_specs=pl.BlockSpec(memory_space=pl.ANY),
        scratch_shapes=[pltpu.VMEM((2, tm, dh), jnp.uint32), pltpu.VMEM((2, tm, dh), jnp.uint32),
                        pltpu.SemaphoreType.DMA((2,)), pltpu.SemaphoreType.DMA((2,))],
    )
    return pl.pallas_call(
        functools.partial(_experts_kernel, tm),
        grid_spec=grid_spec,
        out_shape=jax.ShapeDtypeStruct((TOP_K * n, dh), jnp.uint32),
        compiler_params=_cparams(("arbitrary",)),
        name="moe_experts",
    )(te, n_real, n_valid, tok, tok, dst, h, w_in, b_in, w_out, b_out)


def _combine_kernel(final, y_ref, w_ref, x_ref, gt_ref, *rest):
    if final:
        gf_ref, o_ref = rest
    else:
        (o_ref,) = rest
    w = w_ref[...]
    half = y_ref.shape[2]
    acc_lo = acc_hi = None
    for k in range(TOP_K):
        lo, hi = _unpack_halves(y_ref[k])
        wk = w[:, k:k + 1]
        acc_lo = wk * lo if k == 0 else acc_lo + wk * lo
        acc_hi = wk * hi if k == 0 else acc_hi + wk * hi
    o_lo = x_ref[:, :half] + gt_ref[:, :half] * acc_lo
    o_hi = x_ref[:, half:] + gt_ref[:, half:] * acc_hi
    if final:
        ms = (jnp.sum(o_lo * o_lo, axis=-1, keepdims=True)
              + jnp.sum(o_hi * o_hi, axis=-1, keepdims=True)) * (1.0 / (2 * half))
        inv = lax.rsqrt(ms + EPS)
        o_lo = o_lo * inv * gf_ref[:, :half]
        o_hi = o_hi * inv * gf_ref[:, half:]
    o_ref[:, :half] = o_lo
    o_ref[:, half:] = o_hi


def _combine_call(y, top_w, x, gate, rows_per_group, g_final=None):
    r, d = x.shape
    tm = _pick(rows_per_group, 256)
    per = rows_per_group // tm
    gidx = (lambda i: (i // per, 0, 0)) if gate.shape[0] > 1 else (lambda i: (0, 0, 0))
    final = g_final is not None
    in_specs = [pl.BlockSpec((TOP_K, tm, d // 2), lambda i: (0, i, 0)),
                pl.BlockSpec((tm, LANES), lambda i: (i, 0)),
                pl.BlockSpec((tm, d), lambda i: (i, 0)),
                pl.BlockSpec((None, 1, d), gidx)]
    args = [y.reshape(TOP_K, r, d // 2), top_w, x, gate]
    if final:
        in_specs.append(pl.BlockSpec((1, d), lambda i: (0, 0)))
        args.append(g_final)
    return pl.pallas_call(
        functools.partial(_combine_kernel, final),
        grid=(r // tm,),
        in_specs=in_specs,
        out_specs=pl.BlockSpec((tm, d), lambda i: (i, 0)),
        out_shape=jax.ShapeDtypeStruct((r, d), F32),
        compiler_params=_cparams(("arbitrary",)),
        name="moe_combine",
    )(*args)


def _tri_blockdiag(blk, chunk, upper):
    i = jnp.arange(blk)
    same = (i[:, None] // chunk) == (i[None, :] // chunk)
    tri = (i[:, None] <= i[None, :]) if upper else (i[:, None] >= i[None, :])
    return jnp.logical_and(same, tri).astype(BF16)


def _layer_params(l, p, dims):
    d, ch, qk, inner, heads_ssd = dims["d"], dims["ch"], dims["qk"], dims["inner"], dims["ssd_heads"]
    w_in = p["w_in"][l]
    xbc = inner + 2 * SSD_GROUPS * SSD_STATE
    sizes = [("glu", 2 * ch), ("q", qk), ("k", qk), ("v", qk), ("r", qk), ("a_f", GLA_RANK),
             ("a_b", GLA_RANK), ("z", inner), ("xbc", xbc), ("dt_f", heads_ssd),
             ("dt_b", heads_ssd), ("gate", N_BRANCH * d)]
    off, o = {}, 0
    for name, s in sizes:
        off[name] = (o, s)
        o += s
    col = lambda name, a=0, b=None: w_in[:, off[name][0] + a: off[name][0] + (off[name][1] if b is None else b)]
    small = jnp.zeros((d, LANES), F32)
    small = small.at[:, SM_AF:SM_AF + GLA_RANK].set(col("a_f"))
    small = small.at[:, SM_AB:SM_AB + GLA_RANK].set(col("a_b"))
    small = small.at[:, SM_DTF:SM_DTF + heads_ssd].set(col("dt_f"))
    small = small.at[:, SM_DTB:SM_DTB + heads_ssd].set(col("dt_b"))
    parts = [col("glu"), col("q"), col("k"), col("v"), col("r"), col("z"),
             col("xbc", 0, inner), col("xbc", inner), col("gate"), small]
    w_r = jnp.concatenate(parts, axis=1)
    n_real = w_r.shape[1]
    tn = dims["tn_in"]
    n_pad = -(-n_real // tn) * tn
    w_r = jnp.pad(w_r, ((0, 0), (0, n_pad - n_real))).astype(BF16)

    def wa_pad(wa, lane0):
        return jnp.zeros((LANES, qk), F32).at[lane0:lane0 + GLA_RANK].set(wa).astype(BF16)

    def lane_row(vec, lane0):
        return jnp.zeros((1, LANES), F32).at[0, lane0:lane0 + heads_ssd].set(vec)

    conv_w = p["ssd_conv_w"][l]
    conv_b = p["ssd_conv_b"][l][None, :]
    wr = p["w_router"][l]
    whi = wr.astype(BF16)
    r1 = wr - whi.astype(F32)
    wmid = r1.astype(BF16)
    wlo = (r1 - wmid.astype(F32)).astype(BF16)
    return dict(
        w_in=w_r,
        conv_w=p["conv_w"][l], conv_b=p["conv_b"][l][None], ln_g=p["conv_ln_g"][l][None],
        ln_b=p["conv_ln_b"][l][None],
        wa_f=wa_pad(p["gla_wa_f"][l], SM_AF), wa_b=wa_pad(p["gla_wa_b"][l], SM_AB),
        ba_f=p["gla_ba_f"][l][None], ba_b=p["gla_ba_b"][l][None],
        gla_g=p["gla_norm_g"][l][None],
        cw_x=conv_w[:, :inner], cw_bc=conv_w[:, inner:], cb_x=conv_b[:, :inner], cb_bc=conv_b[:, inner:],
        dtb=lane_row(p["ssd_dt_bias_f"][l], SM_DTF) + lane_row(p["ssd_dt_bias_b"][l], SM_DTB),
        a_f=lane_row(-jnp.exp(p["ssd_a_log_f"][l]), SM_DTF),
        a_b=lane_row(-jnp.exp(p["ssd_a_log_b"][l]), SM_DTB),
        dskip=jnp.repeat(p["ssd_d"][l], SSD_HEADDIM)[None],
        ssd_g=p["ssd_norm_g"][l][None],
        w_c=p["w_proj_conv"][l].astype(BF16), w_g=p["w_proj_gla"][l].astype(BF16),
        w_s=p["w_proj_ssd"][l].astype(BF16), w_out=p["w_out"][l].astype(BF16),
        wr3=jnp.stack([whi, wmid, wlo]), br=p["b_router"][l][None],
        we_in=p["w_exp_in"][l].astype(BF16), be_in=p["b_exp_in"][l][:, None, :],
        we_out=p["w_exp_out"][l].astype(BF16), be_out=p["b_exp_out"][l][:, None, :],
        g_mix=p["g_mix"][l][None], g_ffn=p["g_ffn"][l][None],
    )


def _expand_matrix(lane0, heads, inner):
    lanes = jnp.arange(LANES)[:, None]
    cols = jnp.arange(inner)[None, :]
    return (lanes == lane0 + cols // SSD_HEADDIM).astype(BF16)


def _stream(x, mod, lp, dims, cols, nseq, seq_len, seg_len, states, full, g_final=None):
    inner, qk = dims["inner"], dims["qk"]
    h = _norm_call(x, lp["g_mix"], mod[1], mod[0], seq_len, BF16)
    proj = _mm_call(h, lp["w_in"], 512, dims["tn_in"])

    blk = _pick(seq_len, 256)
    tri_lo = _tri_blockdiag(blk, CHUNK, False)
    tri_up = _tri_blockdiag(blk, CHUNK, True)
    heads = qk // GLA_DK
    o_f, gs_f = _gla_call(False, proj, cols, lp["wa_f"], lp["ba_f"], tri_lo, states[0], nseq, seq_len, heads)
    y_gla, gs_b = _gla_call(True, proj, cols, lp["wa_b"], lp["ba_b"], tri_up, states[1], nseq, seq_len,
                            heads, extra=(o_f, lp["gla_g"]))

    xs = _ssd_prep_call(proj, cols["xs"] * LANES // inner, inner, lp["cw_x"], lp["cb_x"], nseq, seq_len)
    wbc = 2 * SSD_GROUPS * SSD_STATE
    bc = _ssd_prep_call(proj, cols["bc"] * LANES // wbc, wbc, lp["cw_bc"], lp["cb_bc"], nseq, seq_len)
    full_lo = _tri_blockdiag(blk, blk, False)
    full_up = _tri_blockdiag(blk, blk, True)
    ssd_heads = inner // SSD_HEADDIM
    ex_f = _expand_matrix(SM_DTF, ssd_heads, inner)
    ex_b = _expand_matrix(SM_DTB, ssd_heads, inner)
    scols = dict(small=cols["small"], z=cols["z"] * LANES // inner)
    y_f, hs_f = _ssd_call(False, xs, bc, proj, scols, lp["dtb"], lp["a_f"], full_lo, ex_f, states[2],
                          nseq, seq_len)
    y_ssd, hs_b = _ssd_call(True, xs, bc, proj, scols, lp["dtb"], lp["a_b"], full_up, ex_b, states[3],
                            nseq, seq_len, extra=(y_f, lp["dskip"], lp["ssd_g"]))
    finals = (gs_f, gs_b, hs_f, hs_b)
    if not full:
        return None, finals

    y_conv = _conv_call(proj, lp["conv_w"], lp["conv_b"], lp["ln_g"], lp["ln_b"], seg_len)
    tn = dims["tn_merge"]
    merged = _merge_call(proj, cols["gate"] * LANES // tn, y_conv, y_gla, y_ssd,
                         lp["w_c"], lp["w_g"], lp["w_s"], tn)
    x1 = _out_call(merged, lp["w_out"], x, mod[2], seq_len)
    h2, top_i, top_w = _router_call(x1, lp["g_ffn"], mod[4], mod[3], lp["wr3"], lp["br"], seq_len)
    ne = lp["we_in"].shape[0]
    tm = MOE_TILE
    plan = _dispatch_plan(top_i[:, :TOP_K], ne, tm)
    y = _experts_call(h2, plan, lp["we_in"], lp["be_in"], lp["we_out"], lp["be_out"], tm)
    x2 = _combine_call(y, top_w, x1, mod[5], seq_len, g_final)
    return x2, finals


def kernel(x, c, ctx, c_ctx, g_mix, g_ffn, w_mod, b_mod, w_in, conv_w, conv_b, conv_ln_g, conv_ln_b,
           w_proj_conv, gla_wa_f, gla_ba_f, gla_wa_b, gla_ba_b, gla_norm_g, w_proj_gla, ssd_conv_w,
           ssd_conv_b, ssd_dt_bias_f, ssd_dt_bias_b, ssd_a_log_f, ssd_a_log_b, ssd_d, ssd_norm_g,
           w_proj_ssd, w_out, w_router, b_router, w_exp_in, b_exp_in, w_exp_out, b_exp_out, g_final):
    p = dict(g_mix=g_mix, g_ffn=g_ffn, w_in=w_in, conv_w=conv_w, conv_b=conv_b, conv_ln_g=conv_ln_g,
             conv_ln_b=conv_ln_b, w_proj_conv=w_proj_conv, gla_wa_f=gla_wa_f, gla_ba_f=gla_ba_f,
             gla_wa_b=gla_wa_b, gla_ba_b=gla_ba_b, gla_norm_g=gla_norm_g, w_proj_gla=w_proj_gla,
             ssd_conv_w=ssd_conv_w, ssd_conv_b=ssd_conv_b, ssd_dt_bias_f=ssd_dt_bias_f,
             ssd_dt_bias_b=ssd_dt_bias_b, ssd_a_log_f=ssd_a_log_f, ssd_a_log_b=ssd_a_log_b,
             ssd_d=ssd_d, ssd_norm_g=ssd_norm_g, w_proj_ssd=w_proj_ssd, w_out=w_out,
             w_router=w_router, b_router=b_router, w_exp_in=w_exp_in, b_exp_in=b_exp_in,
             w_exp_out=w_exp_out, b_exp_out=b_exp_out)
    bsz, t, d = x.shape
    ctx_len = ctx.shape[1]
    depth = w_mod.shape[0]
    ch = conv_w.shape[2]
    qk = gla_wa_f.shape[2]
    ssd_heads = ssd_d.shape[1]
    inner = ssd_heads * SSD_HEADDIM
    assert ssd_heads <= 16 and bsz + 1 <= 8
    widths = [("glu", 2 * ch), ("q", qk), ("k", qk), ("v", qk), ("r", qk), ("z", inner), ("xs", inner),
              ("bc", 2 * SSD_GROUPS * SSD_STATE), ("gate", N_BRANCH * d), ("small", LANES)]
    cols, o = {}, 0
    for name, wdt in widths:
        assert o % LANES == 0
        cols[name] = o // LANES
        o += wdt
    tn_merge = math.gcd(cols["gate"] * LANES, _pick(d, 1024, LANES))
    dims = dict(d=d, ch=ch, qk=qk, inner=inner, ssd_heads=ssd_heads,
                tn_in=min(1536, o), tn_merge=tn_merge)
    assert (cols["z"] * LANES) % inner == 0 and (cols["xs"] * LANES) % inner == 0
    assert (cols["bc"] * LANES) % (2 * SSD_GROUPS * SSD_STATE) == 0
    assert (cols["gate"] * LANES) % tn_merge == 0 and (cols["glu"] == 0)

    cvecs = jnp.concatenate([c, c_ctx[None, :]], axis=0)
    mods = _mod_call(cvecs, w_mod, b_mod[:, None, :])

    xl = x.reshape(bsz * t, d)
    xc = ctx.reshape(bsz * ctx_len, d)
    heads = qk // GLA_DK
    zero_g = jnp.zeros((bsz, heads, GLA_DV, GLA_DK), F32)
    zero_s = jnp.zeros((bsz, SSD_GROUPS, inner // SSD_GROUPS, SSD_STATE), F32)
    zeros = (zero_g, zero_g, zero_s, zero_s)
    for l in range(depth):
        lp = _layer_params(l, p, dims)
        m = mods[l].reshape(8, N_MOD, d)
        mod_l = jnp.transpose(m[:bsz], (1, 0, 2))[:, :, None, :]
        mod_c = m[bsz][:, None, None, :]
        last = l == depth - 1
        xc_new, ctx_states = _stream(xc, mod_c, lp, dims, cols, bsz, ctx_len, ctx_len, zeros,
                                     full=not last)
        xl, _ = _stream(xl, mod_l, lp, dims, cols, bsz, t, GRID_W, ctx_states, full=True,
                        g_final=g_final[None] if last else None)
        if not last:
            xc = xc_new
    return xl.reshape(bsz, t, d)
```

```python
import functools
import math

import jax
import jax.numpy as jnp
from jax import lax
from jax.experimental import pallas as pl
from jax.experimental.pallas import tpu as pltpu

F32 = jnp.float32
BF16 = jnp.bfloat16

GRID_W = 64
N_MOD = 6
N_BRANCH = 3
EPS = 1e-6
CHUNK = 64
GLA_DK = 128
GLA_DV = 128
GLA_RANK = 16
GLA_TAU = 16.0
SSD_HEADDIM = 64
SSD_GROUPS = 2
SSD_STATE = 128
TOP_K = 4
SWIGLU_LIMIT = 7.0
SWIGLU_ALPHA = 1.702

LANES = 128
VMEM_LIMIT = 56 * 1024 * 1024
NEG_BIG = -1e30
MOE_TILE = 256

SM_AF, SM_AB, SM_DTF, SM_DTB = 0, 16, 32, 48


def _cparams(sem):
    return pltpu.CompilerParams(dimension_semantics=sem, vmem_limit_bytes=VMEM_LIMIT)


def _pick(n, pref, mult=8):
    if n <= pref:
        return n
    t = (pref // mult) * mult
    while t >= mult:
        if n % t == 0:
            return t
        t -= mult
    return n


def _dot(a, b):
    return jnp.dot(a.astype(BF16), b.astype(BF16), preferred_element_type=F32)


def _dot_nt(a, b):
    return lax.dot_general(a.astype(BF16), b.astype(BF16), (((1,), (1,)), ((), ())),
                           preferred_element_type=F32)


def _dot_tn(a, b):
    return lax.dot_general(a.astype(BF16), b.astype(BF16), (((0,), (0,)), ((), ())),
                           preferred_element_type=F32)


def _split2(x):
    hi = x.astype(BF16)
    lo = (x - hi.astype(F32)).astype(BF16)
    return hi, lo


def _split3(x):
    hi = x.astype(BF16)
    r = x - hi.astype(F32)
    mid = r.astype(BF16)
    lo = (r - mid.astype(F32)).astype(BF16)
    return hi, mid, lo


def _sel_dot(sel, x, passes=3):
    dot = lambda p: jnp.dot(sel, p, preferred_element_type=F32)
    if passes == 2:
        hi, lo = _split2(x)
        return dot(hi) + dot(lo)
    hi, mid, lo = _split3(x)
    return dot(hi) + dot(mid) + dot(lo)


def _dot_sel(x, sel):
    hi, lo = _split2(x)
    return (jnp.dot(hi, sel, preferred_element_type=F32)
            + jnp.dot(lo, sel, preferred_element_type=F32))


def _sigmoid(x):
    return 1.0 / (1.0 + jnp.exp(-x))


def _silu(x):
    return x * _sigmoid(x)


def _softplus(x):
    return jnp.maximum(x, 0.0) + jnp.log(1.0 + jnp.exp(-jnp.abs(x)))


def _log_sigmoid(x):
    return jnp.minimum(x, 0.0) - jnp.log(1.0 + jnp.exp(-jnp.abs(x)))


MOD_COLS = 512


def _mod_kernel(nvec, nk, c_ref, w_ref, b_ref, o_ref, acc_ref):
    k = pl.program_id(1)

    @pl.when(k == 0)
    def _():
        acc_ref[...] = jnp.zeros_like(acc_ref)

    tk, n = w_ref.shape
    reps = MOD_COLS // LANES
    acts = [_silu(c_ref[v]).reshape(tk // 8, 8, LANES) for v in range(nvec)]

    def body(jc, carry):
        cols = pl.ds(pl.multiple_of(jc * MOD_COLS, MOD_COLS), MOD_COLS)
        w3 = w_ref[:, cols].reshape(tk // 8, 8, MOD_COLS)
        for v in range(nvec):
            a3 = jnp.concatenate([acts[v]] * reps, axis=2)
            acc_ref[v, :, cols] += jnp.sum(w3 * a3, axis=0)
        return carry

    lax.fori_loop(0, n // MOD_COLS, body, 0)

    @pl.when(k == nk - 1)
    def _():
        rows = [jnp.sum(acc_ref[v], axis=0, keepdims=True) for v in range(nvec)]
        rows.append(jnp.zeros((8 - nvec, n), F32))
        o_ref[...] = jnp.concatenate(rows, axis=0) + b_ref[...]


def _mod_call(cvecs, w_mod, b_mod):
    nvec, d = cvecs.shape
    nl, _, n = w_mod.shape
    tk = _pick(d, 128)
    assert n % MOD_COLS == 0
    crep = jnp.broadcast_to(cvecs[:, :, None], (nvec, d, LANES))
    return pl.pallas_call(
        functools.partial(_mod_kernel, nvec, d // tk),
        grid=(nl, d // tk),
        in_specs=[pl.BlockSpec((nvec, tk, LANES), lambda l, k: (0, k, 0)),
                  pl.BlockSpec((None, tk, n), lambda l, k: (l, k, 0)),
                  pl.BlockSpec((None, 1, n), lambda l, k: (l, 0, 0))],
        out_specs=pl.BlockSpec((None, 8, n), lambda l, k: (l, 0, 0)),
        out_shape=jax.ShapeDtypeStruct((nl, 8, n), F32),
        scratch_shapes=[pltpu.VMEM((nvec, 8, n), F32)],
        compiler_params=_cparams(("arbitrary", "arbitrary")),
        name="mod_vectors",
    )(crep, w_mod, b_mod)


def _norm_mod(x, g, sc, sh):
    y = x * lax.rsqrt(jnp.mean(x * x, axis=-1, keepdims=True) + EPS) * g
    return y * (1.0 + sc) + sh


def _norm_kernel(x_ref, g_ref, sc_ref, sh_ref, o_ref):
    o_ref[...] = _norm_mod(x_ref[...], g_ref[...], sc_ref[...], sh_ref[...]).astype(o_ref.dtype)


def _norm_call(x, g, sc, sh, rows_per_group, out_dtype):
    r, d = x.shape
    tm = _pick(rows_per_group, 256)
    per = rows_per_group // tm
    ng = sc.shape[0]
    gidx = (lambda i: (i // per, 0, 0)) if ng > 1 else (lambda i: (0, 0, 0))
    return pl.pallas_call(
        _norm_kernel,
        grid=(r // tm,),
        in_specs=[pl.BlockSpec((tm, d), lambda i: (i, 0)),
                  pl.BlockSpec((1, d), lambda i: (0, 0)),
                  pl.BlockSpec((None, 1, d), gidx),
                  pl.BlockSpec((None, 1, d), gidx)],
        out_specs=pl.BlockSpec((tm, d), lambda i: (i, 0)),
        out_shape=jax.ShapeDtypeStruct((r, d), out_dtype),
        compiler_params=_cparams(("arbitrary",)),
        name="norm_mod",
    )(x, g, sc, sh)


def _mm_kernel(a_ref, b_ref, o_ref):
    o_ref[...] = jnp.dot(a_ref[...], b_ref[...], preferred_element_type=F32)


def _mm_call(a, b, tm_pref, tn):
    r, k = a.shape
    n = b.shape[1]
    tm = _pick(r, tm_pref)
    assert n % tn == 0
    return pl.pallas_call(
        _mm_kernel,
        grid=(n // tn, r // tm),
        in_specs=[pl.BlockSpec((tm, k), lambda j, i: (i, 0)),
                  pl.BlockSpec((k, tn), lambda j, i: (0, j))],
        out_specs=pl.BlockSpec((tm, tn), lambda j, i: (i, j)),
        out_shape=jax.ShapeDtypeStruct((r, n), F32),
        compiler_params=_cparams(("arbitrary", "arbitrary")),
        name="in_proj",
    )(a, b)


def _conv_kernel(seg_len, glu_ref, w_ref, b_ref, lg_ref, lb_ref, o_ref):
    blk, ch = o_ref.shape
    kw = w_ref.shape[0]
    half = kw // 2
    u = glu_ref[:, :ch] * _sigmoid(glu_ref[:, ch:])
    pos = lax.broadcasted_iota(jnp.int32, (blk, 1), 0) % seg_len
    acc = jnp.zeros((blk, ch), F32)
    for j in range(kw):
        d = j - half
        shifted = u if d == 0 else pltpu.roll(u, (-d) % blk, 0)
        valid = jnp.logical_and(pos + d >= 0, pos + d < seg_len)
        acc = acc + jnp.where(valid, shifted, 0.0) * w_ref[j:j + 1, :]
    y = acc + b_ref[...]
    mu = jnp.mean(y, axis=-1, keepdims=True)
    yc = y - mu
    var = jnp.mean(yc * yc, axis=-1, keepdims=True)
    z = yc * lax.rsqrt(var + EPS) * lg_ref[...] + lb_ref[...]
    o_ref[...] = _silu(z).astype(o_ref.dtype)


def _conv_call(proj, conv_w, conv_b, ln_g, ln_b, seg_len):
    r = proj.shape[0]
    kw, ch = conv_w.shape
    blk = seg_len * max(1, 256 // seg_len)
    assert r % blk == 0
    vec = lambda: pl.BlockSpec((1, ch), lambda i: (0, 0))
    return pl.pallas_call(
        functools.partial(_conv_kernel, seg_len),
        grid=(r // blk,),
        in_specs=[pl.BlockSpec((blk, 2 * ch), lambda i: (i, 0)),
                  pl.BlockSpec((kw, ch), lambda i: (0, 0)),
                  vec(), vec(), vec()],
        out_specs=pl.BlockSpec((blk, ch), lambda i: (i, 0)),
        out_shape=jax.ShapeDtypeStruct((r, ch), BF16),
        compiler_params=_cparams(("arbitrary",)),
        name="conformer_conv",
    )(proj, conv_w, conv_b, ln_g, ln_b)


def _ssd_prep_kernel(nblk, cur_ref, prev_ref, next_ref, w_ref, b_ref, o_ref):
    blk, ch = o_ref.shape
    kw = w_ref.shape[0]
    half = kw // 2
    i = pl.program_id(1)
    prev = jnp.where(i > 0, prev_ref[...], 0.0)
    nxt = jnp.where(i < nblk - 1, next_ref[...], 0.0)
    ext = jnp.concatenate([prev, cur_ref[...], nxt], axis=0)
    n = blk + 16
    acc = jnp.zeros((blk, ch), F32)
    for j in range(kw):
        d = j - half
        shifted = ext if d == 0 else pltpu.roll(ext, (-d) % n, 0)
        acc = acc + shifted[8:8 + blk, :] * w_ref[j:j + 1, :]
    o_ref[...] = _silu(acc + b_ref[...])


def _ssd_prep_call(proj, col_blk, width, w, b, nseq, seq_len):
    r = proj.shape[0]
    kw = w.shape[0]
    blk = _pick(seq_len, 256)
    nblk = seq_len // blk
    b8 = blk // 8
    nb8 = r // 8
    cur = lambda s, i: (s * nblk + i, col_blk)
    prev = lambda s, i: (jnp.maximum((s * nblk + i) * b8 - 1, 0), col_blk)
    nxt = lambda s, i: (jnp.minimum((s * nblk + i + 1) * b8, nb8 - 1), col_blk)
    return pl.pallas_call(
        functools.partial(_ssd_prep_kernel, nblk),
        grid=(nseq, nblk),
        in_specs=[pl.BlockSpec((blk, width), cur),
                  pl.BlockSpec((8, width), prev),
                  pl.BlockSpec((8, width), nxt),
                  pl.BlockSpec((kw, width), lambda s, i: (0, 0)),
                  pl.BlockSpec((1, width), lambda s, i: (0, 0))],
        out_specs=pl.BlockSpec((blk, width), lambda s, i: (s * nblk + i, 0)),
        out_shape=jax.ShapeDtypeStruct((r, width), F32),
        compiler_params=_cparams(("arbitrary", "arbitrary")),
        name="ssd_conv",
    )(proj, proj, proj, w, b)


def _gla_kernel(rev, last, nblk, *refs):
    if last:
        (q_ref, k_ref, v_ref, sm_ref, wa_ref, ba_ref, tri_ref, s0_ref, of_ref, r_ref, g_ref,
         o_ref, sfin_ref, st_ref, sp_ref) = refs
    else:
        (q_ref, k_ref, v_ref, sm_ref, wa_ref, ba_ref, tri_ref, s0_ref,
         o_ref, sfin_ref, st_ref, sp_ref) = refs
    i = pl.program_id(2)

    @pl.when(i == 0)
    def _():
        st_ref[...] = s0_ref[...]

    blk = q_ref.shape[0]
    hps = q_ref.shape[1] // GLA_DK
    nch = blk // CHUNK
    ref_i = CHUNK // 2 - 1 if rev else CHUNK // 2
    last_i = 0 if rev else CHUNK - 1
    bcast = lambda t: jnp.broadcast_to(t, (nch, CHUNK, GLA_DK)).reshape(blk, GLA_DK)
    row = lax.broadcasted_iota(jnp.int32, (blk, blk), 0)
    col = lax.broadcasted_iota(jnp.int32, (blk, blk), 1)
    same = (row // CHUNK) == (col // CHUNK)
    tri = (row <= col) if rev else (row >= col)
    keep = jnp.logical_and(same, tri)
    sel = tri_ref[...]
    sm = sm_ref[...].astype(BF16)
    order = range(nch - 1, -1, -1) if rev else range(nch)

    for hh in range(hps):
        hs = slice(hh * GLA_DK, (hh + 1) * GLA_DK)
        z = jnp.dot(sm, wa_ref[:, hs], preferred_element_type=F32) + ba_ref[:, hs]
        lg = _log_sigmoid(z) * (1.0 / GLA_TAU)
        g = _sel_dot(sel, lg, passes=2)
        g3 = g.reshape(nch, CHUNK, GLA_DK)
        g_ref_pt = bcast(g3[:, ref_i:ref_i + 1, :])
        g_last3 = g3[:, last_i:last_i + 1, :]
        g_last = bcast(g_last3)

        q = q_ref[:, hs] * (GLA_DK ** -0.5)
        k = k_ref[:, hs]
        v = v_ref[:, hs].astype(BF16)
        qe = q * jnp.exp(g - g_ref_pt)
        ke = k * jnp.exp(g_ref_pt - g)
        att = jnp.where(keep, _dot_nt(qe, ke), 0.0)
        y = _dot(att, v)

        kd = (k * jnp.exp(g_last - g)).astype(BF16)
        qg = (q * jnp.exp(g)).astype(BF16)
        dec = jnp.exp(g_last3)
        st = st_ref[hh]
        for c in order:
            sl = slice(c * CHUNK, (c + 1) * CHUNK)
            sp_ref[hh, c] = st
            st = dec[c] * st + _dot_tn(v[sl, :], kd[sl, :])
        st_ref[hh] = st
        y_off = [_dot_nt(qg[c * CHUNK:(c + 1) * CHUNK, :], sp_ref[hh, c]) for c in range(nch)]
        y = y + jnp.concatenate(y_off, axis=0)

        if last:
            o = of_ref[:, hs] + y
            o = o * lax.rsqrt(jnp.mean(o * o, axis=-1, keepdims=True) + EPS) * g_ref[:, hs]
            o_ref[:, hs] = (o * _silu(r_ref[:, hs])).astype(o_ref.dtype)
        else:
            o_ref[:, hs] = y

    @pl.when(i == nblk - 1)
    def _():
        sfin_ref[...] = st_ref[...]


def _gla_call(rev, proj, cols, wa_pad, ba, tri, s0, nseq, seq_len, heads, extra=None):
    r = proj.shape[0]
    blk = tri.shape[0]
    nblk = seq_len // blk
    last = extra is not None
    hps = next(n for n in (8, 4, 2, 1)
               if heads % n == 0 and all(cols[c] % n == 0 for c in ("q", "k", "v", "r")))
    wid = hps * LANES
    tok = (lambda s, i: s * nblk + (nblk - 1 - i)) if rev else (lambda s, i: s * nblk + i)
    colspec = lambda off: pl.BlockSpec((blk, wid), lambda s, h, i: (tok(s, i), off // hps + h))
    hvec = pl.BlockSpec((1, wid), lambda s, h, i: (0, h))
    state = pl.BlockSpec((None, hps, GLA_DV, GLA_DK), lambda s, h, i: (s, h, 0, 0))
    in_specs = [colspec(cols["q"]), colspec(cols["k"]), colspec(cols["v"]),
                pl.BlockSpec((blk, LANES), lambda s, h, i: (tok(s, i), cols["small"])),
                pl.BlockSpec((LANES, wid), lambda s, h, i: (0, h)),
                hvec,
                pl.BlockSpec((blk, blk), lambda s, h, i: (0, 0)),
                state]
    args = [proj, proj, proj, proj, wa_pad, ba, tri, s0]
    if last:
        o_f, norm_g = extra
        in_specs += [pl.BlockSpec((blk, wid), lambda s, h, i: (tok(s, i), h)),
                     colspec(cols["r"]), hvec]
        args += [o_f, proj, norm_g]
    out_dtype = BF16 if last else F32
    return pl.pallas_call(
        functools.partial(_gla_kernel, rev, last, nblk),
        grid=(nseq, heads // hps, nblk),
        in_specs=in_specs,
        out_specs=[pl.BlockSpec((blk, wid), lambda s, h, i: (tok(s, i), h)), state],
        out_shape=[jax.ShapeDtypeStruct((r, heads * GLA_DV), out_dtype),
                   jax.ShapeDtypeStruct((nseq, heads, GLA_DV, GLA_DK), F32)],
        scratch_shapes=[pltpu.VMEM((hps, GLA_DV, GLA_DK), F32),
                        pltpu.VMEM((hps, blk // CHUNK, GLA_DV, GLA_DK), F32)],
        compiler_params=_cparams(("arbitrary", "arbitrary", "arbitrary")),
        name="gla_bwd" if rev else "gla_fwd",
    )(*args)


def _ssd_kernel(rev, last, nblk, lane0, *refs):
    if last:
        (xs_ref, bc_ref, sm_ref, dtb_ref, a_ref, tri_ref, ex_ref, h0_ref, yf_ref, z_ref, d_ref,
         ng_ref, o_ref, hfin_ref, h_ref) = refs
    else:
        (xs_ref, bc_ref, sm_ref, dtb_ref, a_ref, tri_ref, ex_ref, h0_ref,
         o_ref, hfin_ref, h_ref) = refs
    i = pl.program_id(1)

    @pl.when(i == 0)
    def _():
        h_ref[...] = h0_ref[...]

    blk, inner = xs_ref.shape
    gsz = inner // SSD_GROUPS
    epg = gsz // SSD_HEADDIM
    ns = SSD_STATE

    dt = _softplus(sm_ref[...] + dtb_ref[...])
    ad = a_ref[...] * dt
    cum = _sel_dot(tri_ref[...], ad)
    tot = cum[0:1, :] if rev else cum[blk - 1:blk, :]
    cum_t = cum.T
    ex = ex_ref[...]
    xs = xs_ref[...]
    xd = (xs * _dot_sel(dt, ex)).astype(BF16)
    xdw = (xs * _dot_sel(dt * jnp.exp(tot - cum), ex)).astype(BF16)
    ecum = _dot_sel(jnp.exp(cum), ex)

    row = lax.broadcasted_iota(jnp.int32, (blk, blk), 0)
    col = lax.broadcasted_iota(jnp.int32, (blk, blk), 1)
    causal = (row <= col) if rev else (row >= col)
    lane = lax.broadcasted_iota(jnp.int32, (blk, LANES), 1)
    low = lane < SSD_HEADDIM

    y_cols = []
    for g in range(SSD_GROUPS):
        bm = bc_ref[:, g * ns:(g + 1) * ns].astype(BF16)
        cm = bc_ref[:, (SSD_GROUPS + g) * ns:(SSD_GROUPS + g + 1) * ns].astype(BF16)
        cb = _dot_nt(cm, bm)
        hg = h_ref[g]
        y_off = _dot_nt(cm, hg)
        st_new = _dot_tn(xdw[:, g * gsz:(g + 1) * gsz], bm)
        for pr in range(epg // 2):
            c0 = g * gsz + pr * LANES
            xpair = xd[:, c0:c0 + LANES]
            acc = y_off[:, pr * LANES:(pr + 1) * LANES] * ecum[:, c0:c0 + LANES]
            for half in range(2):
                ln = lane0 + g * epg + 2 * pr + half
                seg = cum[:, ln:ln + 1] - cum_t[ln:ln + 1, :]
                w = (cb * jnp.exp(jnp.where(causal, seg, NEG_BIG))).astype(BF16)
                keep = low if half == 0 else jnp.logical_not(low)
                rhs = jnp.where(keep, xpair, jnp.zeros_like(xpair))
                acc = acc + jnp.dot(w, rhs, preferred_element_type=F32)
            y_cols.append(acc)
        for e in range(epg):
            ln = lane0 + g * epg + e
            rs = slice(e * SSD_HEADDIM, (e + 1) * SSD_HEADDIM)
            dec = jnp.broadcast_to(jnp.exp(tot[:, ln:ln + 1]), (SSD_HEADDIM, ns))
            h_ref[g, rs, :] = dec * hg[rs, :] + st_new[rs, :]
    y = jnp.concatenate(y_cols, axis=1)

    @pl.when(i == nblk - 1)
    def _():
        hfin_ref[...] = h_ref[...]

    if last:
        ys = (yf_ref[...] + y + d_ref[...] * xs) * _silu(z_ref[...])
        outs = []
        for g in range(SSD_GROUPS):
            yg = ys[:, g * gsz:(g + 1) * gsz]
            outs.append(yg * lax.rsqrt(jnp.mean(yg * yg, axis=-1, keepdims=True) + EPS))
        o_ref[...] = (jnp.concatenate(outs, axis=1) * ng_ref[...]).astype(o_ref.dtype)
    else:
        o_ref[...] = y


def _ssd_call(rev, xs, bc, proj, cols, dtb, a_row, tri, ex, h0, nseq, seq_len, extra=None):
    r, inner = xs.shape
    blk = tri.shape[0]
    nblk = seq_len // blk
    last = extra is not None
    gsz = inner // SSD_GROUPS
    lane0 = SM_DTB if rev else SM_DTF
    tok = (lambda s, i: s * nblk + (nblk - 1 - i)) if rev else (lambda s, i: s * nblk + i)
    rowblk = lambda w: pl.BlockSpec((blk, w), lambda s, i: (tok(s, i), 0))
    const = lambda shp: pl.BlockSpec(shp, lambda s, i: (0,) * len(shp))
    state = pl.BlockSpec((None, SSD_GROUPS, gsz, SSD_STATE), lambda s, i: (s, 0, 0, 0))
    in_specs = [rowblk(inner), rowblk(bc.shape[1]),
                pl.BlockSpec((blk, LANES), lambda s, i: (tok(s, i), cols["small"])),
                const((1, LANES)), const((1, LANES)), const((blk, blk)), const((LANES, inner)),
                state]
    args = [xs, bc, proj, dtb, a_row, tri, ex, h0]
    if last:
        y_f, dskip, norm_g = extra
        in_specs += [rowblk(inner),
                     pl.BlockSpec((blk, inner), lambda s, i: (tok(s, i), cols["z"])),
                     const((1, inner)), const((1, inner))]
        args += [y_f, proj, dskip, norm_g]
    return pl.pallas_call(
        functools.partial(_ssd_kernel, rev, last, nblk, lane0),
        grid=(nseq, nblk),
        in_specs=in_specs,
        out_specs=[rowblk(inner), state],
        out_shape=[jax.ShapeDtypeStruct((r, inner), BF16 if last else F32),
                   jax.ShapeDtypeStruct((nseq, SSD_GROUPS, gsz, SSD_STATE), F32)],
        scratch_shapes=[pltpu.VMEM((SSD_GROUPS, gsz, SSD_STATE), F32)],
        compiler_params=_cparams(("arbitrary", "arbitrary")),
        name="ssd_bwd" if rev else "ssd_fwd",
    )(*args)


def _merge_kernel(gc_ref, gg_ref, gs_ref, yc_ref, yg_ref, ys_ref, wc_ref, wg_ref, ws_ref, o_ref):
    dot = lambda a, b: jnp.dot(a[...], b[...], preferred_element_type=F32)
    acc = _sigmoid(gc_ref[...]) * dot(yc_ref, wc_ref)
    acc = acc + _sigmoid(gg_ref[...]) * dot(yg_ref, wg_ref)
    acc = acc + _sigmoid(gs_ref[...]) * dot(ys_ref, ws_ref)
    o_ref[...] = acc.astype(o_ref.dtype)


def _merge_call(proj, gate_col, y_conv, y_gla, y_ssd, w_c, w_g, w_s, tn):
    r = proj.shape[0]
    d = w_c.shape[1]
    tm = _pick(r, 1024)
    nj = d // tn
    gate = lambda b: pl.BlockSpec((tm, tn), lambda i, j: (i, gate_col + b * nj + j))
    yspec = lambda y: pl.BlockSpec((tm, y.shape[1]), lambda i, j: (i, 0))
    wspec = lambda w: pl.BlockSpec((w.shape[0], tn), lambda i, j: (0, j))
    return pl.pallas_call(
        _merge_kernel,
        grid=(r // tm, nj),
        in_specs=[gate(0), gate(1), gate(2), yspec(y_conv), yspec(y_gla), yspec(y_ssd),
                  wspec(w_c), wspec(w_g), wspec(w_s)],
        out_specs=pl.BlockSpec((tm, tn), lambda i, j: (i, j)),
        out_shape=jax.ShapeDtypeStruct((r, d), BF16),
        compiler_params=_cparams(("arbitrary", "arbitrary")),
        name="gated_merge",
    )(proj, proj, proj, y_conv, y_gla, y_ssd, w_c, w_g, w_s)


def _out_kernel(m_ref, w_ref, x_ref, gt_ref, o_ref):
    o_ref[...] = x_ref[...] + gt_ref[...] * jnp.dot(m_ref[...], w_ref[...],
                                                     preferred_element_type=F32)


def _out_call(merged, w_out, x, gate, rows_per_group):
    r, d = x.shape
    tm = _pick(rows_per_group, 512)
    tn = _pick(d, 1024, LANES)
    per = rows_per_group // tm
    gidx = (lambda j, i: (i // per, 0, j)) if gate.shape[0] > 1 else (lambda j, i: (0, 0, j))
    return pl.pallas_call(
        _out_kernel,
        grid=(d // tn, r // tm),
        in_specs=[pl.BlockSpec((tm, d), lambda j, i: (i, 0)),
                  pl.BlockSpec((d, tn), lambda j, i: (0, j)),
                  pl.BlockSpec((tm, tn), lambda j, i: (i, j)),
                  pl.BlockSpec((None, 1, tn), gidx)],
        out_specs=pl.BlockSpec((tm, tn), lambda j, i: (i, j)),
        out_shape=jax.ShapeDtypeStruct((r, d), F32),
        compiler_params=_cparams(("arbitrary", "arbitrary")),
        name="out_proj",
    )(merged, w_out, x, gate)


def _router_kernel(x_ref, g_ref, sc_ref, sh_ref, wr_ref, br_ref, h_ref, idx_ref, w_ref):
    h = _norm_mod(x_ref[...], g_ref[...], sc_ref[...], sh_ref[...])
    hhi, hlo = _split2(h)
    whi, wmid, wlo = wr_ref[0], wr_ref[1], wr_ref[2]
    dot = lambda a, b: jnp.dot(a, b, preferred_element_type=F32)
    logits = (dot(hhi, whi) + dot(hhi, wmid) + dot(hlo, whi) + dot(hhi, wlo) + dot(hlo, wmid)
              + br_ref[...])
    tm, ne = logits.shape
    lane = lax.broadcasted_iota(jnp.int32, (tm, ne), 1).astype(F32)
    work = logits
    idxs, vals = [], []
    for _ in range(TOP_K):
        m = jnp.max(work, axis=-1, keepdims=True)
        idx = jnp.min(jnp.where(work == m, lane, float(ne)), axis=-1, keepdims=True)
        idxs.append(idx)
        vals.append(m)
        work = jnp.where(lane == idx, -jnp.inf, work)
    es = [jnp.exp(v - vals[0]) for v in vals]
    inv = 1.0 / sum(es)
    out_lane = lax.broadcasted_iota(jnp.int32, (tm, LANES), 1)
    idx_out = jnp.zeros((tm, LANES), F32)
    w_out = jnp.zeros((tm, LANES), F32)
    for k in range(TOP_K):
        idx_out = jnp.where(out_lane == k, idxs[k], idx_out)
        w_out = jnp.where(out_lane == k, es[k] * inv, w_out)
    idx_ref[...] = idx_out.astype(jnp.int32)
    w_ref[...] = w_out
    h_ref[...] = _pack_halves(h)


def _pack_halves(x):
    half = x.shape[1] // 2
    bits = lambda t: lax.bitcast_convert_type(t.astype(BF16).astype(F32), jnp.uint32)
    return (bits(x[:, half:]) & jnp.uint32(0xFFFF0000)) | (bits(x[:, :half]) >> 16)


def _unpack_halves(w):
    lo = lax.bitcast_convert_type(w << 16, F32)
    hi = lax.bitcast_convert_type(w & jnp.uint32(0xFFFF0000), F32)
    return lo, hi


def _router_call(x, g, sc, sh, wr3, br, rows_per_group):
    r, d = x.shape
    ne = wr3.shape[2]
    tm = _pick(rows_per_group, 256)
    per = rows_per_group // tm
    gidx = (lambda i: (i // per, 0, 0)) if sc.shape[0] > 1 else (lambda i: (0, 0, 0))
    row = lambda w: pl.BlockSpec((tm, w), lambda i: (i, 0))
    return pl.pallas_call(
        _router_kernel,
        grid=(r // tm,),
        in_specs=[row(d),
                  pl.BlockSpec((1, d), lambda i: (0, 0)),
                  pl.BlockSpec((None, 1, d), gidx),
                  pl.BlockSpec((None, 1, d), gidx),
                  pl.BlockSpec((3, d, ne), lambda i: (0, 0, 0)),
                  pl.BlockSpec((1, ne), lambda i: (0, 0))],
        out_specs=[row(d // 2), row(LANES), row(LANES)],
        out_shape=[jax.ShapeDtypeStruct((r, d // 2), jnp.uint32),
                   jax.ShapeDtypeStruct((r, LANES), jnp.int32),
                   jax.ShapeDtypeStruct((r, LANES), F32)],
        compiler_params=_cparams(("arbitrary",)),
        name="router",
    )(x, g, sc, sh, wr3, br)


def _dispatch_plan(top_i, ne, tm):
    n, topk = top_i.shape
    n_tiles = (n * topk) // tm + ne
    onehot = (top_i[:, :, None] == jnp.arange(ne, dtype=jnp.int32)[None, None, :]).any(axis=1)
    onehot = onehot.astype(jnp.int32)
    cnt = onehot.sum(axis=0)
    rank = jnp.cumsum(onehot, axis=0) - onehot
    tiles_e = (cnt + tm - 1) // tm
    tile_end = jnp.cumsum(tiles_e)
    tile_start = tile_end - tiles_e
    n_valid = tile_end[-1]
    slot = tile_start[top_i] * tm + jnp.take_along_axis(rank, top_i, axis=1)
    tok = jnp.arange(n, dtype=jnp.int32)[:, None]
    dst = tok + jnp.arange(topk, dtype=jnp.int32)[None, :] * n
    dst_of_slot = jnp.zeros((n_tiles * tm,), jnp.int32).at[slot.reshape(-1)].set(dst.reshape(-1))
    tok_of_slot = dst_of_slot % n
    t = jnp.arange(n_tiles, dtype=jnp.int32)
    te = jnp.sum((tile_end[None, :] <= t[:, None]).astype(jnp.int32), axis=1)
    te = jnp.minimum(te, ne - 1)
    n_real = jnp.clip(cnt[te] - (t - tile_start[te]) * tm, 0, tm).astype(jnp.int32)
    return (te, n_real, n_valid.astype(jnp.int32).reshape(1),
            tok_of_slot.reshape(n_tiles, 1, tm), dst_of_slot.reshape(n_tiles, 1, tm))


def _experts_kernel(tm, te_ref, nr_ref, nv_ref, tok_ref, tokn_ref, dst_ref, h_hbm, wi_ref, bi_ref,
                    wo_ref, bo_ref, y_hbm, gbuf, ybuf, gsem, ssem):
    t = pl.program_id(0)
    nv = nv_ref[0]
    slot = t % 2

    def row_gather(ids_ref, s, r):
        return pltpu.make_async_copy(h_hbm.at[pl.ds(ids_ref[0, r], 1)], gbuf.at[s, pl.ds(r, 1)],
                                     gsem.at[s])

    def row_scatter(s, r, dst_row):
        return pltpu.make_async_copy(ybuf.at[s, pl.ds(r, 1)], y_hbm.at[pl.ds(dst_row, 1)],
                                     ssem.at[s])

    def wait_scatter(s, n):
        @pl.when(n == tm)
        def _():
            pltpu.make_async_copy(ybuf.at[s], y_hbm.at[pl.ds(0, tm)], ssem.at[s]).wait()

        @pl.when(n != tm)
        def _():
            def body(r, c):
                row_scatter(s, 0, 0).wait()
                return c
            lax.fori_loop(0, n, body, 0)

    @pl.when(t == 0)
    def _():
        def body(r, c):
            row_gather(tok_ref, 0, r).start()
            return c
        lax.fori_loop(0, tm, body, 0, unroll=8)

    @pl.when(t + 1 < nv)
    def _():
        for r in range(tm):
            row_gather(tokn_ref, 1 - slot, r).start()

    @pl.when(t < nv)
    def _():
        pltpu.make_async_copy(h_hbm.at[pl.ds(0, tm)], gbuf.at[slot], gsem.at[slot]).wait()

        @pl.when(t >= 2)
        def _():
            wait_scatter(slot, nr_ref[t - 2])

        ff = wo_ref.shape[0]
        half = wi_ref.shape[0] // 2
        lo, hi = _unpack_halves(gbuf[slot])
        gu = (jnp.dot(lo.astype(BF16), wi_ref[:half, :], preferred_element_type=F32)
              + jnp.dot(hi.astype(BF16), wi_ref[half:, :], preferred_element_type=F32) + bi_ref[...])
        gate = jnp.minimum(gu[:, :ff], SWIGLU_LIMIT)
        up = jnp.clip(gu[:, ff:], -SWIGLU_LIMIT, SWIGLU_LIMIT)
        act = gate * _sigmoid(SWIGLU_ALPHA * gate) * (up + 1.0)
        y = jnp.dot(act.astype(BF16), wo_ref[...], preferred_element_type=F32) + bo_ref[...]
        ybuf[slot] = _pack_halves(y)

        n_real = nr_ref[t]

        @pl.when(n_real == tm)
        def _():
            for r in range(tm):
                row_scatter(slot, r, dst_ref[0, r]).start()

        @pl.when(n_real != tm)
        def _():
            def body(r, c):
                row_scatter(slot, r, dst_ref[0, r]).start()
                return c
            lax.fori_loop(0, n_real, body, 0)

        @pl.when(t == nv - 1)
        def _():
            wait_scatter(slot, n_real)

            @pl.when(t >= 1)
            def _():
                wait_scatter(1 - slot, nr_ref[t - 1])


def _experts_call(h, plan, w_in, b_in, w_out, b_out, tm):
    n, dh = h.shape
    d = 2 * dh
    te, n_real, n_valid, tok, dst = plan
    n_tiles = te.shape[0]
    ne, _, ff2 = w_in.shape
    ff = ff2 // 2
    ids = lambda f: pl.BlockSpec((None, 1, tm), f, memory_space=pltpu.SMEM)
    grid_spec = pltpu.PrefetchScalarGridSpec(
        num_scalar_prefetch=3,
        grid=(n_tiles,),
        in_specs=[ids(lambda t, te, nr, nv: (t, 0, 0)),
                  ids(lambda t, te, nr, nv: (jnp.minimum(t + 1, n_tiles - 1), 0, 0)),
                  ids(lambda t, te, nr, nv: (t, 0, 0)),
                  pl.BlockSpec(memory_space=pl.ANY),
                  pl.BlockSpec((None, d, ff2), lambda t, te, nr, nv: (te[t], 0, 0)),
                  pl.BlockSpec((None, 1, ff2), lambda t, te, nr, nv: (te[t], 0, 0)),
                  pl.BlockSpec((None, ff, d), lambda t, te, nr, nv: (te[t], 0, 0)),
                  pl.BlockSpec((None, 1, d), lambda t, te, nr, nv: (te[t], 0, 0))],
        out_specs=pl.BlockSpec(memory_space=pl.ANY),
        scratch_shapes=[pltpu.VMEM((2, tm, dh), jnp.uint32), pltpu.VMEM((2, tm, dh), jnp.uint32),
                        pltpu.SemaphoreType.DMA((2,)), pltpu.SemaphoreType.DMA((2,))],
    )
    return pl.pallas_call(
        functools.partial(_experts_kernel, tm),
        grid_spec=grid_spec,
        out_shape=jax.ShapeDtypeStruct((TOP_K * n, dh), jnp.uint32),
        compiler_params=_cparams(("arbitrary",)),
        name="moe_experts",
    )(te, n_real, n_valid, tok, tok, dst, h, w_in, b_in, w_out, b_out)


def _combine_kernel(final, y_ref, w_ref, x_ref, gt_ref, *rest):
    if final:
        gf_ref, o_ref = rest
    else:
        (o_ref,) = rest
    w = w_ref[...]
    half = y_ref.shape[2]
    acc_lo = acc_hi = None
    for k in range(TOP_K):
        lo, hi = _unpack_halves(y_ref[k])
        wk = w[:, k:k + 1]
        acc_lo = wk * lo if k == 0 else acc_lo + wk * lo
        acc_hi = wk * hi if k == 0 else acc_hi + wk * hi
    o_lo = x_ref[:, :half] + gt_ref[:, :half] * acc_lo
    o_hi = x_ref[:, half:] + gt_ref[:, half:] * acc_hi
    if final:
        ms = (jnp.sum(o_lo * o_lo, axis=-1, keepdims=True)
              + jnp.sum(o_hi * o_hi, axis=-1, keepdims=True)) * (1.0 / (2 * half))
        inv = lax.rsqrt(ms + EPS)
        o_lo = o_lo * inv * gf_ref[:, :half]
        o_hi = o_hi * inv * gf_ref[:, half:]
    o_ref[:, :half] = o_lo
    o_ref[:, half:] = o_hi


def _combine_call(y, top_w, x, gate, rows_per_group, g_final=None):
    r, d = x.shape
    tm = _pick(rows_per_group, 256)
    per = rows_per_group // tm
    gidx = (lambda i: (i // per, 0, 0)) if gate.shape[0] > 1 else (lambda i: (0, 0, 0))
    final = g_final is not None
    in_specs = [pl.BlockSpec((TOP_K, tm, d // 2), lambda i: (0, i, 0)),
                pl.BlockSpec((tm, LANES), lambda i: (i, 0)),
                pl.BlockSpec((tm, d), lambda i: (i, 0)),
                pl.BlockSpec((None, 1, d), gidx)]
    args = [y.reshape(TOP_K, r, d // 2), top_w, x, gate]
    if final:
        in_specs.append(pl.BlockSpec((1, d), lambda i: (0, 0)))
        args.append(g_final)
    return pl.pallas_call(
        functools.partial(_combine_kernel, final),
        grid=(r // tm,),
        in_specs=in_specs,
        out_specs=pl.BlockSpec((tm, d), lambda i: (i, 0)),
        out_shape=jax.ShapeDtypeStruct((r, d), F32),
        compiler_params=_cparams(("arbitrary",)),
        name="moe_combine",
    )(*args)


def _tri_blockdiag(blk, chunk, upper):
    i = jnp.arange(blk)
    same = (i[:, None] // chunk) == (i[None, :] // chunk)
    tri = (i[:, None] <= i[None, :]) if upper else (i[:, None] >= i[None, :])
    return jnp.logical_and(same, tri).astype(BF16)


def _layer_params(l, p, dims):
    d, ch, qk, inner, heads_ssd = dims["d"], dims["ch"], dims["qk"], dims["inner"], dims["ssd_heads"]
    w_in = p["w_in"][l]
    xbc = inner + 2 * SSD_GROUPS * SSD_STATE
    sizes = [("glu", 2 * ch), ("q", qk), ("k", qk), ("v", qk), ("r", qk), ("a_f", GLA_RANK),
             ("a_b", GLA_RANK), ("z", inner), ("xbc", xbc), ("dt_f", heads_ssd),
             ("dt_b", heads_ssd), ("gate", N_BRANCH * d)]
    off, o = {}, 0
    for name, s in sizes:
        off[name] = (o, s)
        o += s
    col = lambda name, a=0, b=None: w_in[:, off[name][0] + a: off[name][0] + (off[name][1] if b is None else b)]
    small = jnp.zeros((d, LANES), F32)
    small = small.at[:, SM_AF:SM_AF + GLA_RANK].set(col("a_f"))
    small = small.at[:, SM_AB:SM_AB + GLA_RANK].set(col("a_b"))
    small = small.at[:, SM_DTF:SM_DTF + heads_ssd].set(col("dt_f"))
    small = small.at[:, SM_DTB:SM_DTB + heads_ssd].set(col("dt_b"))
    parts = [col("glu"), col("q"), col("k"), col("v"), col("r"), col("z"),
             col("xbc", 0, inner), col("xbc", inner), col("gate"), small]
    w_r = jnp.concatenate(parts, axis=1)
    n_real = w_r.shape[1]
    tn = dims["tn_in"]
    n_pad = -(-n_real // tn) * tn
    w_r = jnp.pad(w_r, ((0, 0), (0, n_pad - n_real))).astype(BF16)

    def wa_pad(wa, lane0):
        return jnp.zeros((LANES, qk), F32).at[lane0:lane0 + GLA_RANK].set(wa).astype(BF16)

    def lane_row(vec, lane0):
        return jnp.zeros((1, LANES), F32).at[0, lane0:lane0 + heads_ssd].set(vec)

    conv_w = p["ssd_conv_w"][l]
    conv_b = p["ssd_conv_b"][l][None, :]
    wr = p["w_router"][l]
    whi = wr.astype(BF16)
    r1 = wr - whi.astype(F32)
    wmid = r1.astype(BF16)
    wlo = (r1 - wmid.astype(F32)).astype(BF16)
    return dict(
        w_in=w_r,
        conv_w=p["conv_w"][l], conv_b=p["conv_b"][l][None], ln_g=p["conv_ln_g"][l][None],
        ln_b=p["conv_ln_b"][l][None],
        wa_f=wa_pad(p["gla_wa_f"][l], SM_AF), wa_b=wa_pad(p["gla_wa_b"][l], SM_AB),
        ba_f=p["gla_ba_f"][l][None], ba_b=p["gla_ba_b"][l][None],
        gla_g=p["gla_norm_g"][l][None],
        cw_x=conv_w[:, :inner], cw_bc=conv_w[:, inner:], cb_x=conv_b[:, :inner], cb_bc=conv_b[:, inner:],
        dtb=lane_row(p["ssd_dt_bias_f"][l], SM_DTF) + lane_row(p["ssd_dt_bias_b"][l], SM_DTB),
        a_f=lane_row(-jnp.exp(p["ssd_a_log_f"][l]), SM_DTF),
        a_b=lane_row(-jnp.exp(p["ssd_a_log_b"][l]), SM_DTB),
        dskip=jnp.repeat(p["ssd_d"][l], SSD_HEADDIM)[None],
        ssd_g=p["ssd_norm_g"][l][None],
        w_c=p["w_proj_conv"][l].astype(BF16), w_g=p["w_proj_gla"][l].astype(BF16),
        w_s=p["w_proj_ssd"][l].astype(BF16), w_out=p["w_out"][l].astype(BF16),
        wr3=jnp.stack([whi, wmid, wlo]), br=p["b_router"][l][None],
        we_in=p["w_exp_in"][l].astype(BF16), be_in=p["b_exp_in"][l][:, None, :],
        we_out=p["w_exp_out"][l].astype(BF16), be_out=p["b_exp_out"][l][:, None, :],
        g_mix=p["g_mix"][l][None], g_ffn=p["g_ffn"][l][None],
    )


def _expand_matrix(lane0, heads, inner):
    lanes = jnp.arange(LANES)[:, None]
    cols = jnp.arange(inner)[None, :]
    return (lanes == lane0 + cols // SSD_HEADDIM).astype(BF16)


def _stream(x, mod, lp, dims, cols, nseq, seq_len, seg_len, states, full, g_final=None):
    inner, qk = dims["inner"], dims["qk"]
    h = _norm_call(x, lp["g_mix"], mod[1], mod[0], seq_len, BF16)
    proj = _mm_call(h, lp["w_in"], 512, dims["tn_in"])

    blk = _pick(seq_len, 256)
    tri_lo = _tri_blockdiag(blk, CHUNK, False)
    tri_up = _tri_blockdiag(blk, CHUNK, True)
    heads = qk // GLA_DK
    o_f, gs_f = _gla_call(False, proj, cols, lp["wa_f"], lp["ba_f"], tri_lo, states[0], nseq, seq_len, heads)
    y_gla, gs_b = _gla_call(True, proj, cols, lp["wa_b"], lp["ba_b"], tri_up, states[1], nseq, seq_len,
                            heads, extra=(o_f, lp["gla_g"]))

    xs = _ssd_prep_call(proj, cols["xs"] * LANES // inner, inner, lp["cw_x"], lp["cb_x"], nseq, seq_len)
    wbc = 2 * SSD_GROUPS * SSD_STATE
    bc = _ssd_prep_call(proj, cols["bc"] * LANES // wbc, wbc, lp["cw_bc"], lp["cb_bc"], nseq, seq_len)
    full_lo = _tri_blockdiag(blk, blk, False)
    full_up = _tri_blockdiag(blk, blk, True)
    ssd_heads = inner // SSD_HEADDIM
    ex_f = _expand_matrix(SM_DTF, ssd_heads, inner)
    ex_b = _expand_matrix(SM_DTB, ssd_heads, inner)
    scols = dict(small=cols["small"], z=cols["z"] * LANES // inner)
    y_f, hs_f = _ssd_call(False, xs, bc, proj, scols, lp["dtb"], lp["a_f"], full_lo, ex_f, states[2],
                          nseq, seq_len)
    y_ssd, hs_b = _ssd_call(True, xs, bc, proj, scols, lp["dtb"], lp["a_b"], full_up, ex_b, states[3],
                            nseq, seq_len, extra=(y_f, lp["dskip"], lp["ssd_g"]))
    finals = (gs_f, gs_b, hs_f, hs_b)
    if not full:
        return None, finals

    y_conv = _conv_call(proj, lp["conv_w"], lp["conv_b"], lp["ln_g"], lp["ln_b"], seg_len)
    tn = dims["tn_merge"]
    merged = _merge_call(proj, cols["gate"] * LANES // tn, y_conv, y_gla, y_ssd,
                         lp["w_c"], lp["w_g"], lp["w_s"], tn)
    x1 = _out_call(merged, lp["w_out"], x, mod[2], seq_len)
    h2, top_i, top_w = _router_call(x1, lp["g_ffn"], mod[4], mod[3], lp["wr3"], lp["br"], seq_len)
    ne = lp["we_in"].shape[0]
    per_expert = max(64, x.shape[0] * TOP_K // ne)
    tm = min(MOE_TILE, 1 << (per_expert.bit_length() - 1))
    plan = _dispatch_plan(top_i[:, :TOP_K], ne, tm)
    y = _experts_call(h2, plan, lp["we_in"], lp["be_in"], lp["we_out"], lp["be_out"], tm)
    x2 = _combine_call(y, top_w, x1, mod[5], seq_len, g_final)
    return x2, finals


def kernel(x, c, ctx, c_ctx, g_mix, g_ffn, w_mod, b_mod, w_in, conv_w, conv_b, conv_ln_g, conv_ln_b,
           w_proj_conv, gla_wa_f, gla_ba_f, gla_wa_b, gla_ba_b, gla_norm_g, w_proj_gla, ssd_conv_w,
           ssd_conv_b, ssd_dt_bias_f, ssd_dt_bias_b, ssd_a_log_f, ssd_a_log_b, ssd_d, ssd_norm_g,
           w_proj_ssd, w_out, w_router, b_router, w_exp_in, b_exp_in, w_exp_out, b_exp_out, g_final):
    p = dict(g_mix=g_mix, g_ffn=g_ffn, w_in=w_in, conv_w=conv_w, conv_b=conv_b, conv_ln_g=conv_ln_g,
             conv_ln_b=conv_ln_b, w_proj_conv=w_proj_conv, gla_wa_f=gla_wa_f, gla_ba_f=gla_ba_f,
             gla_wa_b=gla_wa_b, gla_ba_b=gla_ba_b, gla_norm_g=gla_norm_g, w_proj_gla=w_proj_gla,
             ssd_conv_w=ssd_conv_w, ssd_conv_b=ssd_conv_b, ssd_dt_bias_f=ssd_dt_bias_f,
             ssd_dt_bias_b=ssd_dt_bias_b, ssd_a_log_f=ssd_a_log_f, ssd_a_log_b=ssd_a_log_b,
             ssd_d=ssd_d, ssd_norm_g=ssd_norm_g, w_proj_ssd=w_proj_ssd, w_out=w_out,
             w_router=w_router, b_router=b_router, w_exp_in=w_exp_in, b_exp_in=b_exp_in,
             w_exp_out=w_exp_out, b_exp_out=b_exp_out)
    bsz, t, d = x.shape
    ctx_len = ctx.shape[1]
    depth = w_mod.shape[0]
    ch = conv_w.shape[2]
    qk = gla_wa_f.shape[2]
    ssd_heads = ssd_d.shape[1]
    inner = ssd_heads * SSD_HEADDIM
    assert ssd_heads <= 16 and bsz + 1 <= 8
    widths = [("glu", 2 * ch), ("q", qk), ("k", qk), ("v", qk), ("r", qk), ("z", inner), ("xs", inner),
              ("bc", 2 * SSD_GROUPS * SSD_STATE), ("gate", N_BRANCH * d), ("small", LANES)]
    cols, o = {}, 0
    for name, wdt in widths:
        assert o % LANES == 0
        cols[name] = o // LANES
        o += wdt
    tn_merge = math.gcd(cols["gate"] * LANES, _pick(d, 1024, LANES))
    dims = dict(d=d, ch=ch, qk=qk, inner=inner, ssd_heads=ssd_heads,
                tn_in=min(1536, o), tn_merge=tn_merge)
    assert (cols["z"] * LANES) % inner == 0 and (cols["xs"] * LANES) % inner == 0
    assert (cols["bc"] * LANES) % (2 * SSD_GROUPS * SSD_STATE) == 0
    assert (cols["gate"] * LANES) % tn_merge == 0 and (cols["glu"] == 0)

    cvecs = jnp.concatenate([c, c_ctx[None, :]], axis=0)
    mods = _mod_call(cvecs, w_mod, b_mod[:, None, :])

    xl = x.reshape(bsz * t, d)
    xc = ctx.reshape(bsz * ctx_len, d)
    heads = qk // GLA_DK
    zero_g = jnp.zeros((bsz, heads, GLA_DV, GLA_DK), F32)
    zero_s = jnp.zeros((bsz, SSD_GROUPS, inner // SSD_GROUPS, SSD_STATE), F32)
    zeros = (zero_g, zero_g, zero_s, zero_s)
    for l in range(depth):
        lp = _layer_params(l, p, dims)
        m = mods[l].reshape(8, N_MOD, d)
        mod_l = jnp.transpose(m[:bsz], (1, 0, 2))[:, :, None, :]
        mod_c = m[bsz][:, None, None, :]
        last = l == depth - 1
        xc_new, ctx_states = _stream(xc, mod_c, lp, dims, cols, bsz, ctx_len, ctx_len, zeros,
                                     full=not last)
        xl, _ = _stream(xl, mod_l, lp, dims, cols, bsz, t, GRID_W, ctx_states, full=True,
                        g_final=g_final[None] if last else None)
        if not last:
            xc = xc_new
    return xl.reshape(bsz, t, d)
```

```python
import functools
import math

import jax
import jax.numpy as jnp
from jax import lax
from jax.experimental import pallas as pl
from jax.experimental.pallas import tpu as pltpu

F32 = jnp.float32
BF16 = jnp.bfloat16

GRID_W = 64
N_MOD = 6
N_BRANCH = 3
EPS = 1e-6
CHUNK = 64
GLA_DK = 128
GLA_DV = 128
GLA_RANK = 16
GLA_TAU = 16.0
SSD_HEADDIM = 64
SSD_GROUPS = 2
SSD_STATE = 128
TOP_K = 4
SWIGLU_LIMIT = 7.0
SWIGLU_ALPHA = 1.702

LANES = 128
VMEM_LIMIT = 56 * 1024 * 1024
NEG_BIG = -1e30
MOE_TILE = 256

SM_AF, SM_AB, SM_DTF, SM_DTB = 0, 16, 32, 48


def _cparams(sem):
    return pltpu.CompilerParams(dimension_semantics=sem, vmem_limit_bytes=VMEM_LIMIT)


def _pick(n, pref, mult=8):
    if n <= pref:
        return n
    t = (pref // mult) * mult
    while t >= mult:
        if n % t == 0:
            return t
        t -= mult
    return n


def _dot(a, b):
    return jnp.dot(a.astype(BF16), b.astype(BF16), preferred_element_type=F32)


def _dot_nt(a, b):
    return lax.dot_general(a.astype(BF16), b.astype(BF16), (((1,), (1,)), ((), ())),
                           preferred_element_type=F32)


def _dot_tn(a, b):
    return lax.dot_general(a.astype(BF16), b.astype(BF16), (((0,), (0,)), ((), ())),
                           preferred_element_type=F32)


def _split2(x):
    hi = x.astype(BF16)
    lo = (x - hi.astype(F32)).astype(BF16)
    return hi, lo


def _split3(x):
    hi = x.astype(BF16)
    r = x - hi.astype(F32)
    mid = r.astype(BF16)
    lo = (r - mid.astype(F32)).astype(BF16)
    return hi, mid, lo


def _sel_dot(sel, x, passes=3):
    dot = lambda p: jnp.dot(sel, p, preferred_element_type=F32)
    if passes == 2:
        hi, lo = _split2(x)
        return dot(hi) + dot(lo)
    hi, mid, lo = _split3(x)
    return dot(hi) + dot(mid) + dot(lo)


def _dot_sel(x, sel):
    hi, lo = _split2(x)
    return (jnp.dot(hi, sel, preferred_element_type=F32)
            + jnp.dot(lo, sel, preferred_element_type=F32))


def _sigmoid(x):
    return 1.0 / (1.0 + jnp.exp(-x))


def _silu(x):
    return x * _sigmoid(x)


def _softplus(x):
    return jnp.maximum(x, 0.0) + jnp.log(1.0 + jnp.exp(-jnp.abs(x)))


def _log_sigmoid(x):
    return jnp.minimum(x, 0.0) - jnp.log(1.0 + jnp.exp(-jnp.abs(x)))


MOD_COLS = 512
MOD_STREAMS = 4


def _mod_kernel(nvec, nk, c_ref, *refs):
    w_refs = refs[:MOD_STREAMS]
    b_ref, o_ref, acc_ref = refs[MOD_STREAMS:]
    k = pl.program_id(1)

    @pl.when(k == 0)
    def _():
        acc_ref[...] = jnp.zeros_like(acc_ref)

    tk = c_ref.shape[1]
    n = w_refs[0].shape[1]
    reps = MOD_COLS // LANES
    acts = [_silu(c_ref[v]).reshape(tk // 8, 8, LANES) for v in range(nvec)]

    def body(jc, carry):
        cols = pl.ds(pl.multiple_of(jc * MOD_COLS, MOD_COLS), MOD_COLS)
        w3 = jnp.concatenate([w[:, cols] for w in w_refs], axis=0).reshape(tk // 8, 8, MOD_COLS)
        for v in range(nvec):
            a3 = jnp.concatenate([acts[v]] * reps, axis=2)
            acc_ref[v, :, cols] += jnp.sum(w3 * a3, axis=0)
        return carry

    lax.fori_loop(0, n // MOD_COLS, body, 0)

    @pl.when(k == nk - 1)
    def _():
        rows = [jnp.sum(acc_ref[v], axis=0, keepdims=True) for v in range(nvec)]
        rows.append(jnp.zeros((8 - nvec, n), F32))
        o_ref[...] = jnp.concatenate(rows, axis=0) + b_ref[...]


def _mod_call(cvecs, w_mod, b_mod):
    nvec, d = cvecs.shape
    nl, _, n = w_mod.shape
    tk = _pick(d, 128)
    sub = tk // MOD_STREAMS
    assert n % MOD_COLS == 0 and sub % 8 == 0
    crep = jnp.broadcast_to(cvecs[:, :, None], (nvec, d, LANES))
    piece = lambda q: pl.BlockSpec((None, sub, n), lambda l, k: (l, k * MOD_STREAMS + q, 0))
    return pl.pallas_call(
        functools.partial(_mod_kernel, nvec, d // tk),
        grid=(nl, d // tk),
        in_specs=[pl.BlockSpec((nvec, tk, LANES), lambda l, k: (0, k, 0))]
        + [piece(q) for q in range(MOD_STREAMS)]
        + [pl.BlockSpec((None, 1, n), lambda l, k: (l, 0, 0))],
        out_specs=pl.BlockSpec((None, 8, n), lambda l, k: (l, 0, 0)),
        out_shape=jax.ShapeDtypeStruct((nl, 8, n), F32),
        scratch_shapes=[pltpu.VMEM((nvec, 8, n), F32)],
        compiler_params=_cparams(("arbitrary", "arbitrary")),
        name="mod_vectors",
    )(crep, *([w_mod] * MOD_STREAMS), b_mod)


def _norm_mod(x, g, sc, sh):
    y = x * lax.rsqrt(jnp.mean(x * x, axis=-1, keepdims=True) + EPS) * g
    return y * (1.0 + sc) + sh


def _norm_kernel(x_ref, g_ref, sc_ref, sh_ref, o_ref):
    o_ref[...] = _norm_mod(x_ref[...], g_ref[...], sc_ref[...], sh_ref[...]).astype(o_ref.dtype)


def _norm_call(x, g, sc, sh, rows_per_group, out_dtype):
    r, d = x.shape
    tm = _pick(rows_per_group, 256)
    per = rows_per_group // tm
    ng = sc.shape[0]
    gidx = (lambda i: (i // per, 0, 0)) if ng > 1 else (lambda i: (0, 0, 0))
    return pl.pallas_call(
        _norm_kernel,
        grid=(r // tm,),
        in_specs=[pl.BlockSpec((tm, d), lambda i: (i, 0)),
                  pl.BlockSpec((1, d), lambda i: (0, 0)),
                  pl.BlockSpec((None, 1, d), gidx),
                  pl.BlockSpec((None, 1, d), gidx)],
        out_specs=pl.BlockSpec((tm, d), lambda i: (i, 0)),
        out_shape=jax.ShapeDtypeStruct((r, d), out_dtype),
        compiler_params=_cparams(("arbitrary",)),
        name="norm_mod",
    )(x, g, sc, sh)


def _mm_kernel(a_ref, b_ref, o_ref):
    o_ref[...] = jnp.dot(a_ref[...], b_ref[...], preferred_element_type=F32)


def _mm_call(a, b, tm_pref, tn):
    r, k = a.shape
    n = b.shape[1]
    tm = _pick(r, tm_pref)
    assert n % tn == 0
    return pl.pallas_call(
        _mm_kernel,
        grid=(n // tn, r // tm),
        in_specs=[pl.BlockSpec((tm, k), lambda j, i: (i, 0)),
                  pl.BlockSpec((k, tn), lambda j, i: (0, j))],
        out_specs=pl.BlockSpec((tm, tn), lambda j, i: (i, j)),
        out_shape=jax.ShapeDtypeStruct((r, n), F32),
        compiler_params=_cparams(("arbitrary", "arbitrary")),
        name="in_proj",
    )(a, b)


def _conv_kernel(seg_len, glu_ref, w_ref, b_ref, lg_ref, lb_ref, o_ref):
    blk, ch = o_ref.shape
    kw = w_ref.shape[0]
    half = kw // 2
    u = glu_ref[:, :ch] * _sigmoid(glu_ref[:, ch:])
    pos = lax.broadcasted_iota(jnp.int32, (blk, 1), 0) % seg_len
    acc = jnp.zeros((blk, ch), F32)
    for j in range(kw):
        d = j - half
        shifted = u if d == 0 else pltpu.roll(u, (-d) % blk, 0)
        valid = jnp.logical_and(pos + d >= 0, pos + d < seg_len)
        acc = acc + jnp.where(valid, shifted, 0.0) * w_ref[j:j + 1, :]
    y = acc + b_ref[...]
    mu = jnp.mean(y, axis=-1, keepdims=True)
    yc = y - mu
    var = jnp.mean(yc * yc, axis=-1, keepdims=True)
    z = yc * lax.rsqrt(var + EPS) * lg_ref[...] + lb_ref[...]
    o_ref[...] = _silu(z).astype(o_ref.dtype)


def _conv_call(proj, conv_w, conv_b, ln_g, ln_b, seg_len):
    r = proj.shape[0]
    kw, ch = conv_w.shape
    blk = seg_len * max(1, 256 // seg_len)
    assert r % blk == 0
    vec = lambda: pl.BlockSpec((1, ch), lambda i: (0, 0))
    return pl.pallas_call(
        functools.partial(_conv_kernel, seg_len),
        grid=(r // blk,),
        in_specs=[pl.BlockSpec((blk, 2 * ch), lambda i: (i, 0)),
                  pl.BlockSpec((kw, ch), lambda i: (0, 0)),
                  vec(), vec(), vec()],
        out_specs=pl.BlockSpec((blk, ch), lambda i: (i, 0)),
        out_shape=jax.ShapeDtypeStruct((r, ch), BF16),
        compiler_params=_cparams(("arbitrary",)),
        name="conformer_conv",
    )(proj, conv_w, conv_b, ln_g, ln_b)


def _ssd_prep_kernel(nblk, cur_ref, prev_ref, next_ref, w_ref, b_ref, o_ref):
    blk, ch = o_ref.shape
    kw = w_ref.shape[0]
    half = kw // 2
    i = pl.program_id(1)
    prev = jnp.where(i > 0, prev_ref[...], 0.0)
    nxt = jnp.where(i < nblk - 1, next_ref[...], 0.0)
    ext = jnp.concatenate([prev, cur_ref[...], nxt], axis=0)
    n = blk + 16
    acc = jnp.zeros((blk, ch), F32)
    for j in range(kw):
        d = j - half
        shifted = ext if d == 0 else pltpu.roll(ext, (-d) % n, 0)
        acc = acc + shifted[8:8 + blk, :] * w_ref[j:j + 1, :]
    o_ref[...] = _silu(acc + b_ref[...])


def _ssd_prep_call(proj, col_blk, width, w, b, nseq, seq_len):
    r = proj.shape[0]
    kw = w.shape[0]
    blk = _pick(seq_len, 256)
    nblk = seq_len // blk
    b8 = blk // 8
    nb8 = r // 8
    cur = lambda s, i: (s * nblk + i, col_blk)
    prev = lambda s, i: (jnp.maximum((s * nblk + i) * b8 - 1, 0), col_blk)
    nxt = lambda s, i: (jnp.minimum((s * nblk + i + 1) * b8, nb8 - 1), col_blk)
    return pl.pallas_call(
        functools.partial(_ssd_prep_kernel, nblk),
        grid=(nseq, nblk),
        in_specs=[pl.BlockSpec((blk, width), cur),
                  pl.BlockSpec((8, width), prev),
                  pl.BlockSpec((8, width), nxt),
                  pl.BlockSpec((kw, width), lambda s, i: (0, 0)),
                  pl.BlockSpec((1, width), lambda s, i: (0, 0))],
        out_specs=pl.BlockSpec((blk, width), lambda s, i: (s * nblk + i, 0)),
        out_shape=jax.ShapeDtypeStruct((r, width), F32),
        compiler_params=_cparams(("arbitrary", "arbitrary")),
        name="ssd_conv",
    )(proj, proj, proj, w, b)


def _gla_kernel(rev, last, nblk, *refs):
    if last:
        (q_ref, k_ref, v_ref, sm_ref, wa_ref, ba_ref, tri_ref, s0_ref, of_ref, r_ref, g_ref,
         o_ref, sfin_ref, st_ref, sp_ref) = refs
    else:
        (q_ref, k_ref, v_ref, sm_ref, wa_ref, ba_ref, tri_ref, s0_ref,
         o_ref, sfin_ref, st_ref, sp_ref) = refs
    i = pl.program_id(2)

    @pl.when(i == 0)
    def _():
        st_ref[...] = s0_ref[...]

    blk = q_ref.shape[0]
    hps = q_ref.shape[1] // GLA_DK
    nch = blk // CHUNK
    ref_i = CHUNK // 2 - 1 if rev else CHUNK // 2
    last_i = 0 if rev else CHUNK - 1
    bcast = lambda t: jnp.broadcast_to(t, (nch, CHUNK, GLA_DK)).reshape(blk, GLA_DK)
    row = lax.broadcasted_iota(jnp.int32, (blk, blk), 0)
    col = lax.broadcasted_iota(jnp.int32, (blk, blk), 1)
    same = (row // CHUNK) == (col // CHUNK)
    tri = (row <= col) if rev else (row >= col)
    keep = jnp.logical_and(same, tri)
    sel = tri_ref[...]
    sm = sm_ref[...].astype(BF16)
    order = range(nch - 1, -1, -1) if rev else range(nch)

    for hh in range(hps):
        hs = slice(hh * GLA_DK, (hh + 1) * GLA_DK)
        z = jnp.dot(sm, wa_ref[:, hs], preferred_element_type=F32) + ba_ref[:, hs]
        lg = _log_sigmoid(z) * (1.0 / GLA_TAU)
        g = _sel_dot(sel, lg, passes=2)
        g3 = g.reshape(nch, CHUNK, GLA_DK)
        g_ref_pt = bcast(g3[:, ref_i:ref_i + 1, :])
        g_last3 = g3[:, last_i:last_i + 1, :]
        g_last = bcast(g_last3)

        q = q_ref[:, hs] * (GLA_DK ** -0.5)
        k = k_ref[:, hs]
        v = v_ref[:, hs].astype(BF16)
        qe = q * jnp.exp(g - g_ref_pt)
        ke = k * jnp.exp(g_ref_pt - g)
        att = jnp.where(keep, _dot_nt(qe, ke), 0.0)
        y = _dot(att, v)

        kd = (k * jnp.exp(g_last - g)).astype(BF16)
        qg = (q * jnp.exp(g)).astype(BF16)
        dec = jnp.exp(g_last3)
        st = st_ref[hh]
        for c in order:
            sl = slice(c * CHUNK, (c + 1) * CHUNK)
            sp_ref[hh, c] = st
            st = dec[c] * st + _dot_tn(v[sl, :], kd[sl, :])
        st_ref[hh] = st
        y_off = [_dot_nt(qg[c * CHUNK:(c + 1) * CHUNK, :], sp_ref[hh, c]) for c in range(nch)]
        y = y + jnp.concatenate(y_off, axis=0)

        if last:
            o = of_ref[:, hs] + y
            o = o * lax.rsqrt(jnp.mean(o * o, axis=-1, keepdims=True) + EPS) * g_ref[:, hs]
            o_ref[:, hs] = (o * _silu(r_ref[:, hs])).astype(o_ref.dtype)
        else:
            o_ref[:, hs] = y

    @pl.when(i == nblk - 1)
    def _():
        sfin_ref[...] = st_ref[...]


def _gla_call(rev, proj, cols, wa_pad, ba, tri, s0, nseq, seq_len, heads, extra=None):
    r = proj.shape[0]
    blk = tri.shape[0]
    nblk = seq_len // blk
    last = extra is not None
    hps = next(n for n in (8, 4, 2, 1)
               if heads % n == 0 and all(cols[c] % n == 0 for c in ("q", "k", "v", "r")))
    wid = hps * LANES
    tok = (lambda s, i: s * nblk + (nblk - 1 - i)) if rev else (lambda s, i: s * nblk + i)
    colspec = lambda off: pl.BlockSpec((blk, wid), lambda s, h, i: (tok(s, i), off // hps + h))
    hvec = pl.BlockSpec((1, wid), lambda s, h, i: (0, h))
    state = pl.BlockSpec((None, hps, GLA_DV, GLA_DK), lambda s, h, i: (s, h, 0, 0))
    in_specs = [colspec(cols["q"]), colspec(cols["k"]), colspec(cols["v"]),
                pl.BlockSpec((blk, LANES), lambda s, h, i: (tok(s, i), cols["small"])),
                pl.BlockSpec((LANES, wid), lambda s, h, i: (0, h)),
                hvec,
                pl.BlockSpec((blk, blk), lambda s, h, i: (0, 0)),
                state]
    args = [proj, proj, proj, proj, wa_pad, ba, tri, s0]
    if last:
        o_f, norm_g = extra
        in_specs += [pl.BlockSpec((blk, wid), lambda s, h, i: (tok(s, i), h)),
                     colspec(cols["r"]), hvec]
        args += [o_f, proj, norm_g]
    out_dtype = BF16 if last else F32
    return pl.pallas_call(
        functools.partial(_gla_kernel, rev, last, nblk),
        grid=(nseq, heads // hps, nblk),
        in_specs=in_specs,
        out_specs=[pl.BlockSpec((blk, wid), lambda s, h, i: (tok(s, i), h)), state],
        out_shape=[jax.ShapeDtypeStruct((r, heads * GLA_DV), out_dtype),
                   jax.ShapeDtypeStruct((nseq, heads, GLA_DV, GLA_DK), F32)],
        scratch_shapes=[pltpu.VMEM((hps, GLA_DV, GLA_DK), F32),
                        pltpu.VMEM((hps, blk // CHUNK, GLA_DV, GLA_DK), F32)],
        compiler_params=_cparams(("arbitrary", "arbitrary", "arbitrary")),
        name="gla_bwd" if rev else "gla_fwd",
    )(*args)


def _ssd_kernel(rev, last, nblk, lane0, *refs):
    if last:
        (xs_ref, bc_ref, sm_ref, dtb_ref, a_ref, tri_ref, ex_ref, h0_ref, yf_ref, z_ref, d_ref,
         ng_ref, o_ref, hfin_ref, h_ref) = refs
    else:
        (xs_ref, bc_ref, sm_ref, dtb_ref, a_ref, tri_ref, ex_ref, h0_ref,
         o_ref, hfin_ref, h_ref) = refs
    i = pl.program_id(1)

    @pl.when(i == 0)
    def _():
        h_ref[...] = h0_ref[...]

    blk, inner = xs_ref.shape
    gsz = inner // SSD_GROUPS
    epg = gsz // SSD_HEADDIM
    ns = SSD_STATE

    dt = _softplus(sm_ref[...] + dtb_ref[...])
    ad = a_ref[...] * dt
    cum = _sel_dot(tri_ref[...], ad)
    tot = cum[0:1, :] if rev else cum[blk - 1:blk, :]
    cum_t = cum.T
    ex = ex_ref[...]
    xs = xs_ref[...]
    xd = (xs * _dot_sel(dt, ex)).astype(BF16)
    xdw = (xs * _dot_sel(dt * jnp.exp(tot - cum), ex)).astype(BF16)
    ecum = _dot_sel(jnp.exp(cum), ex)

    row = lax.broadcasted_iota(jnp.int32, (blk, blk), 0)
    col = lax.broadcasted_iota(jnp.int32, (blk, blk), 1)
    causal = (row <= col) if rev else (row >= col)
    lane = lax.broadcasted_iota(jnp.int32, (blk, LANES), 1)
    low = lane < SSD_HEADDIM

    y_cols = []
    for g in range(SSD_GROUPS):
        bm = bc_ref[:, g * ns:(g + 1) * ns].astype(BF16)
        cm = bc_ref[:, (SSD_GROUPS + g) * ns:(SSD_GROUPS + g + 1) * ns].astype(BF16)
        cb = _dot_nt(cm, bm)
        hg = h_ref[g]
        y_off = _dot_nt(cm, hg)
        st_new = _dot_tn(xdw[:, g * gsz:(g + 1) * gsz], bm)
        for pr in range(epg // 2):
            c0 = g * gsz + pr * LANES
            xpair = xd[:, c0:c0 + LANES]
            acc = y_off[:, pr * LANES:(pr + 1) * LANES] * ecum[:, c0:c0 + LANES]
            for half in range(2):
                ln = lane0 + g * epg + 2 * pr + half
                seg = cum[:, ln:ln + 1] - cum_t[ln:ln + 1, :]
                w = (cb * jnp.exp(jnp.where(causal, seg, NEG_BIG))).astype(BF16)
                keep = low if half == 0 else jnp.logical_not(low)
                rhs = jnp.where(keep, xpair, jnp.zeros_like(xpair))
                acc = acc + jnp.dot(w, rhs, preferred_element_type=F32)
            y_cols.append(acc)
        for e in range(epg):
            ln = lane0 + g * epg + e
            rs = slice(e * SSD_HEADDIM, (e + 1) * SSD_HEADDIM)
            dec = jnp.broadcast_to(jnp.exp(tot[:, ln:ln + 1]), (SSD_HEADDIM, ns))
            h_ref[g, rs, :] = dec * hg[rs, :] + st_new[rs, :]
    y = jnp.concatenate(y_cols, axis=1)

    @pl.when(i == nblk - 1)
    def _():
        hfin_ref[...] = h_ref[...]

    if last:
        ys = (yf_ref[...] + y + d_ref[...] * xs) * _silu(z_ref[...])
        outs = []
        for g in range(SSD_GROUPS):
            yg = ys[:, g * gsz:(g + 1) * gsz]
            outs.append(yg * lax.rsqrt(jnp.mean(yg * yg, axis=-1, keepdims=True) + EPS))
        o_ref[...] = (jnp.concatenate(outs, axis=1) * ng_ref[...]).astype(o_ref.dtype)
    else:
        o_ref[...] = y


def _ssd_call(rev, xs, bc, proj, cols, dtb, a_row, tri, ex, h0, nseq, seq_len, extra=None):
    r, inner = xs.shape
    blk = tri.shape[0]
    nblk = seq_len // blk
    last = extra is not None
    gsz = inner // SSD_GROUPS
    lane0 = SM_DTB if rev else SM_DTF
    tok = (lambda s, i: s * nblk + (nblk - 1 - i)) if rev else (lambda s, i: s * nblk + i)
    rowblk = lambda w: pl.BlockSpec((blk, w), lambda s, i: (tok(s, i), 0))
    const = lambda shp: pl.BlockSpec(shp, lambda s, i: (0,) * len(shp))
    state = pl.BlockSpec((None, SSD_GROUPS, gsz, SSD_STATE), lambda s, i: (s, 0, 0, 0))
    in_specs = [rowblk(inner), rowblk(bc.shape[1]),
                pl.BlockSpec((blk, LANES), lambda s, i: (tok(s, i), cols["small"])),
                const((1, LANES)), const((1, LANES)), const((blk, blk)), const((LANES, inner)),
                state]
    args = [xs, bc, proj, dtb, a_row, tri, ex, h0]
    if last:
        y_f, dskip, norm_g = extra
        in_specs += [rowblk(inner),
                     pl.BlockSpec((blk, inner), lambda s, i: (tok(s, i), cols["z"])),
                     const((1, inner)), const((1, inner))]
        args += [y_f, proj, dskip, norm_g]
    return pl.pallas_call(
        functools.partial(_ssd_kernel, rev, last, nblk, lane0),
        grid=(nseq, nblk),
        in_specs=in_specs,
        out_specs=[rowblk(inner), state],
        out_shape=[jax.ShapeDtypeStruct((r, inner), BF16 if last else F32),
                   jax.ShapeDtypeStruct((nseq, SSD_GROUPS, gsz, SSD_STATE), F32)],
        scratch_shapes=[pltpu.VMEM((SSD_GROUPS, gsz, SSD_STATE), F32)],
        compiler_params=_cparams(("arbitrary", "arbitrary")),
        name="ssd_bwd" if rev else "ssd_fwd",
    )(*args)


def _merge_kernel(gc_ref, gg_ref, gs_ref, yc_ref, yg_ref, ys_ref, wc_ref, wg_ref, ws_ref, o_ref):
    dot = lambda a, b: jnp.dot(a[...], b[...], preferred_element_type=F32)
    acc = _sigmoid(gc_ref[...]) * dot(yc_ref, wc_ref)
    acc = acc + _sigmoid(gg_ref[...]) * dot(yg_ref, wg_ref)
    acc = acc + _sigmoid(gs_ref[...]) * dot(ys_ref, ws_ref)
    o_ref[...] = acc.astype(o_ref.dtype)


def _merge_call(proj, gate_col, y_conv, y_gla, y_ssd, w_c, w_g, w_s, tn):
    r = proj.shape[0]
    d = w_c.shape[1]
    tm = _pick(r, 1024)
    nj = d // tn
    gate = lambda b: pl.BlockSpec((tm, tn), lambda i, j: (i, gate_col + b * nj + j))
    yspec = lambda y: pl.BlockSpec((tm, y.shape[1]), lambda i, j: (i, 0))
    wspec = lambda w: pl.BlockSpec((w.shape[0], tn), lambda i, j: (0, j))
    return pl.pallas_call(
        _merge_kernel,
        grid=(r // tm, nj),
        in_specs=[gate(0), gate(1), gate(2), yspec(y_conv), yspec(y_gla), yspec(y_ssd),
                  wspec(w_c), wspec(w_g), wspec(w_s)],
        out_specs=pl.BlockSpec((tm, tn), lambda i, j: (i, j)),
        out_shape=jax.ShapeDtypeStruct((r, d), BF16),
        compiler_params=_cparams(("arbitrary", "arbitrary")),
        name="gated_merge",
    )(proj, proj, proj, y_conv, y_gla, y_ssd, w_c, w_g, w_s)


def _out_kernel(m_ref, w_ref, x_ref, gt_ref, o_ref):
    o_ref[...] = x_ref[...] + gt_ref[...] * jnp.dot(m_ref[...], w_ref[...],
                                                     preferred_element_type=F32)


def _out_call(merged, w_out, x, gate, rows_per_group):
    r, d = x.shape
    tm = _pick(rows_per_group, 512)
    tn = _pick(d, 1024, LANES)
    per = rows_per_group // tm
    gidx = (lambda j, i: (i // per, 0, j)) if gate.shape[0] > 1 else (lambda j, i: (0, 0, j))
    return pl.pallas_call(
        _out_kernel,
        grid=(d // tn, r // tm),
        in_specs=[pl.BlockSpec((tm, d), lambda j, i: (i, 0)),
                  pl.BlockSpec((d, tn), lambda j, i: (0, j)),
                  pl.BlockSpec((tm, tn), lambda j, i: (i, j)),
                  pl.BlockSpec((None, 1, tn), gidx)],
        out_specs=pl.BlockSpec((tm, tn), lambda j, i: (i, j)),
        out_shape=jax.ShapeDtypeStruct((r, d), F32),
        compiler_params=_cparams(("arbitrary", "arbitrary")),
        name="out_proj",
    )(merged, w_out, x, gate)


def _router_kernel(x_ref, g_ref, sc_ref, sh_ref, wr_ref, br_ref, h_ref, idx_ref, w_ref):
    h = _norm_mod(x_ref[...], g_ref[...], sc_ref[...], sh_ref[...])
    hhi, hlo = _split2(h)
    whi, wmid, wlo = wr_ref[0], wr_ref[1], wr_ref[2]
    dot = lambda a, b: jnp.dot(a, b, preferred_element_type=F32)
    logits = (dot(hhi, whi) + dot(hhi, wmid) + dot(hlo, whi) + dot(hhi, wlo) + dot(hlo, wmid)
              + br_ref[...])
    tm, ne = logits.shape
    lane = lax.broadcasted_iota(jnp.int32, (tm, ne), 1).astype(F32)
    work = logits
    idxs, vals = [], []
    for _ in range(TOP_K):
        m = jnp.max(work, axis=-1, keepdims=True)
        idx = jnp.min(jnp.where(work == m, lane, float(ne)), axis=-1, keepdims=True)
        idxs.append(idx)
        vals.append(m)
        work = jnp.where(lane == idx, -jnp.inf, work)
    es = [jnp.exp(v - vals[0]) for v in vals]
    inv = 1.0 / sum(es)
    out_lane = lax.broadcasted_iota(jnp.int32, (tm, LANES), 1)
    idx_out = jnp.zeros((tm, LANES), F32)
    w_out = jnp.zeros((tm, LANES), F32)
    for k in range(TOP_K):
        idx_out = jnp.where(out_lane == k, idxs[k], idx_out)
        w_out = jnp.where(out_lane == k, es[k] * inv, w_out)
    idx_ref[...] = idx_out.astype(jnp.int32)
    w_ref[...] = w_out
    h_ref[...] = _pack_halves(h)


def _pack_halves(x):
    half = x.shape[1] // 2
    bits = lambda t: lax.bitcast_convert_type(t.astype(BF16).astype(F32), jnp.uint32)
    return (bits(x[:, half:]) & jnp.uint32(0xFFFF0000)) | (bits(x[:, :half]) >> 16)


def _unpack_halves(w):
    lo = lax.bitcast_convert_type(w << 16, F32)
    hi = lax.bitcast_convert_type(w & jnp.uint32(0xFFFF0000), F32)
    return lo, hi


def _router_call(x, g, sc, sh, wr3, br, rows_per_group):
    r, d = x.shape
    ne = wr3.shape[2]
    tm = _pick(rows_per_group, 256)
    per = rows_per_group // tm
    gidx = (lambda i: (i // per, 0, 0)) if sc.shape[0] > 1 else (lambda i: (0, 0, 0))
    row = lambda w: pl.BlockSpec((tm, w), lambda i: (i, 0))
    return pl.pallas_call(
        _router_kernel,
        grid=(r // tm,),
        in_specs=[row(d),
                  pl.BlockSpec((1, d), lambda i: (0, 0)),
                  pl.BlockSpec((None, 1, d), gidx),
                  pl.BlockSpec((None, 1, d), gidx),
                  pl.BlockSpec((3, d, ne), lambda i: (0, 0, 0)),
                  pl.BlockSpec((1, ne), lambda i: (0, 0))],
        out_specs=[row(d // 2), row(LANES), row(LANES)],
        out_shape=[jax.ShapeDtypeStruct((r, d // 2), jnp.uint32),
                   jax.ShapeDtypeStruct((r, LANES), jnp.int32),
                   jax.ShapeDtypeStruct((r, LANES), F32)],
        compiler_params=_cparams(("arbitrary",)),
        name="router",
    )(x, g, sc, sh, wr3, br)


def _dispatch_plan(top_i, ne, tm):
    n, topk = top_i.shape
    n_tiles = (n * topk) // tm + ne
    onehot = (top_i[:, :, None] == jnp.arange(ne, dtype=jnp.int32)[None, None, :]).any(axis=1)
    onehot = onehot.astype(jnp.int32)
    cnt = onehot.sum(axis=0)
    rank = jnp.cumsum(onehot, axis=0) - onehot
    tiles_e = (cnt + tm - 1) // tm
    tile_end = jnp.cumsum(tiles_e)
    tile_start = tile_end - tiles_e
    n_valid = tile_end[-1]
    slot = tile_start[top_i] * tm + jnp.take_along_axis(rank, top_i, axis=1)
    tok = jnp.arange(n, dtype=jnp.int32)[:, None]
    dst = tok + jnp.arange(topk, dtype=jnp.int32)[None, :] * n
    dst_of_slot = jnp.zeros((n_tiles * tm,), jnp.int32).at[slot.reshape(-1)].set(dst.reshape(-1))
    tok_of_slot = dst_of_slot % n
    t = jnp.arange(n_tiles, dtype=jnp.int32)
    te = jnp.sum((tile_end[None, :] <= t[:, None]).astype(jnp.int32), axis=1)
    te = jnp.minimum(te, ne - 1)
    n_real = jnp.clip(cnt[te] - (t - tile_start[te]) * tm, 0, tm).astype(jnp.int32)
    return (te, n_real, n_valid.astype(jnp.int32).reshape(1),
            tok_of_slot.reshape(n_tiles, 1, tm), dst_of_slot.reshape(n_tiles, 1, tm))


def _experts_kernel(tm, te_ref, nr_ref, nv_ref, tok_ref, tokn_ref, dst_ref, h_hbm, wi_ref, bi_ref,
                    wo_ref, bo_ref, y_hbm, gbuf, ybuf, gsem, ssem):
    t = pl.program_id(0)
    nv = nv_ref[0]
    slot = t % 2

    def row_gather(ids_ref, s, r):
        return pltpu.make_async_copy(h_hbm.at[pl.ds(ids_ref[0, r], 1)], gbuf.at[s, pl.ds(r, 1)],
                                     gsem.at[s])

    def row_scatter(s, r, dst_row):
        return pltpu.make_async_copy(ybuf.at[s, pl.ds(r, 1)], y_hbm.at[pl.ds(dst_row, 1)],
                                     ssem.at[s])

    def wait_scatter(s, n):
        @pl.when(n == tm)
        def _():
            pltpu.make_async_copy(ybuf.at[s], y_hbm.at[pl.ds(0, tm)], ssem.at[s]).wait()

        @pl.when(n != tm)
        def _():
            def body(r, c):
                row_scatter(s, 0, 0).wait()
                return c
            lax.fori_loop(0, n, body, 0)

    @pl.when(t == 0)
    def _():
        def body(r, c):
            row_gather(tok_ref, 0, r).start()
            return c
        lax.fori_loop(0, tm, body, 0, unroll=8)

    @pl.when(t + 1 < nv)
    def _():
        for r in range(tm):
            row_gather(tokn_ref, 1 - slot, r).start()

    @pl.when(t < nv)
    def _():
        pltpu.make_async_copy(h_hbm.at[pl.ds(0, tm)], gbuf.at[slot], gsem.at[slot]).wait()

        @pl.when(t >= 2)
        def _():
            wait_scatter(slot, nr_ref[t - 2])

        ff = wo_ref.shape[0]
        half = wi_ref.shape[0] // 2
        lo, hi = _unpack_halves(gbuf[slot])
        gu = (jnp.dot(lo.astype(BF16), wi_ref[:half, :], preferred_element_type=F32)
              + jnp.dot(hi.astype(BF16), wi_ref[half:, :], preferred_element_type=F32) + bi_ref[...])
        gate = jnp.minimum(gu[:, :ff], SWIGLU_LIMIT)
        up = jnp.clip(gu[:, ff:], -SWIGLU_LIMIT, SWIGLU_LIMIT)
        act = gate * _sigmoid(SWIGLU_ALPHA * gate) * (up + 1.0)
        y = jnp.dot(act.astype(BF16), wo_ref[...], preferred_element_type=F32) + bo_ref[...]
        ybuf[slot] = _pack_halves(y)

        n_real = nr_ref[t]

        @pl.when(n_real == tm)
        def _():
            for r in range(tm):
                row_scatter(slot, r, dst_ref[0, r]).start()

        @pl.when(n_real != tm)
        def _():
            def body(r, c):
                row_scatter(slot, r, dst_ref[0, r]).start()
                return c
            lax.fori_loop(0, n_real, body, 0)

        @pl.when(t == nv - 1)
        def _():
            wait_scatter(slot, n_real)

            @pl.when(t >= 1)
            def _():
                wait_scatter(1 - slot, nr_ref[t - 1])


def _experts_call(h, plan, w_in, b_in, w_out, b_out, tm):
    n, dh = h.shape
    d = 2 * dh
    te, n_real, n_valid, tok, dst = plan
    n_tiles = te.shape[0]
    ne, _, ff2 = w_in.shape
    ff = ff2 // 2
    ids = lambda f: pl.BlockSpec((None, 1, tm), f, memory_space=pltpu.SMEM)
    grid_spec = pltpu.PrefetchScalarGridSpec(
        num_scalar_prefetch=3,
        grid=(n_tiles,),
        in_specs=[ids(lambda t, te, nr, nv: (t, 0, 0)),
                  ids(lambda t, te, nr, nv: (jnp.minimum(t + 1, n_tiles - 1), 0, 0)),
                  ids(lambda t, te, nr, nv: (t, 0, 0)),
                  pl.BlockSpec(memory_space=pl.ANY),
                  pl.BlockSpec((None, d, ff2), lambda t, te, nr, nv: (te[t], 0, 0)),
                  pl.BlockSpec((None, 1, ff2), lambda t, te, nr, nv: (te[t], 0, 0)),
                  pl.BlockSpec((None, ff, d), lambda t, te, nr, nv: (te[t], 0, 0)),
                  pl.BlockSpec((None, 1, d), lambda t, te, nr, nv: (te[t], 0, 0))],
        out_specs=pl.BlockSpec(memory_space=pl.ANY),
        scratch_shapes=[pltpu.VMEM((2, tm, dh), jnp.uint32), pltpu.VMEM((2, tm, dh), jnp.uint32),
                        pltpu.SemaphoreType.DMA((2,)), pltpu.SemaphoreType.DMA((2,))],
    )
    return pl.pallas_call(
        functools.partial(_experts_kernel, tm),
        grid_spec=grid_spec,
        out_shape=jax.ShapeDtypeStruct((TOP_K * n, dh), jnp.uint32),
        compiler_params=_cparams(("arbitrary",)),
        name="moe_experts",
    )(te, n_real, n_valid, tok, tok, dst, h, w_in, b_in, w_out, b_out)


def _combine_kernel(final, y_ref, w_ref, x_ref, gt_ref, *rest):
    if final:
        gf_ref, o_ref = rest
    else:
        (o_ref,) = rest
    w = w_ref[...]
    half = y_ref.shape[2]
    acc_lo = acc_hi = None
    for k in range(TOP_K):
        lo, hi = _unpack_halves(y_ref[k])
        wk = w[:, k:k + 1]
        acc_lo = wk * lo if k == 0 else acc_lo + wk * lo
        acc_hi = wk * hi if k == 0 else acc_hi + wk * hi
    o_lo = x_ref[:, :half] + gt_ref[:, :half] * acc_lo
    o_hi = x_ref[:, half:] + gt_ref[:, half:] * acc_hi
    if final:
        ms = (jnp.sum(o_lo * o_lo, axis=-1, keepdims=True)
              + jnp.sum(o_hi * o_hi, axis=-1, keepdims=True)) * (1.0 / (2 * half))
        inv = lax.rsqrt(ms + EPS)
        o_lo = o_lo * inv * gf_ref[:, :half]
        o_hi = o_hi * inv * gf_ref[:, half:]
    o_ref[:, :half] = o_lo
    o_ref[:, half:] = o_hi


def _combine_call(y, top_w, x, gate, rows_per_group, g_final=None):
    r, d = x.shape
    tm = _pick(rows_per_group, 256)
    per = rows_per_group // tm
    gidx = (lambda i: (i // per, 0, 0)) if gate.shape[0] > 1 else (lambda i: (0, 0, 0))
    final = g_final is not None
    in_specs = [pl.BlockSpec((TOP_K, tm, d // 2), lambda i: (0, i, 0)),
                pl.BlockSpec((tm, LANES), lambda i: (i, 0)),
                pl.BlockSpec((tm, d), lambda i: (i, 0)),
                pl.BlockSpec((None, 1, d), gidx)]
    args = [y.reshape(TOP_K, r, d // 2), top_w, x, gate]
    if final:
        in_specs.append(pl.BlockSpec((1, d), lambda i: (0, 0)))
        args.append(g_final)
    return pl.pallas_call(
        functools.partial(_combine_kernel, final),
        grid=(r // tm,),
        in_specs=in_specs,
        out_specs=pl.BlockSpec((tm, d), lambda i: (i, 0)),
        out_shape=jax.ShapeDtypeStruct((r, d), F32),
        compiler_params=_cparams(("arbitrary",)),
        name="moe_combine",
    )(*args)


def _tri_blockdiag(blk, chunk, upper):
    i = jnp.arange(blk)
    same = (i[:, None] // chunk) == (i[None, :] // chunk)
    tri = (i[:, None] <= i[None, :]) if upper else (i[:, None] >= i[None, :])
    return jnp.logical_and(same, tri).astype(BF16)


def _layer_params(l, p, dims):
    d, ch, qk, inner, heads_ssd = dims["d"], dims["ch"], dims["qk"], dims["inner"], dims["ssd_heads"]
    w_in = p["w_in"][l]
    xbc = inner + 2 * SSD_GROUPS * SSD_STATE
    sizes = [("glu", 2 * ch), ("q", qk), ("k", qk), ("v", qk), ("r", qk), ("a_f", GLA_RANK),
             ("a_b", GLA_RANK), ("z", inner), ("xbc", xbc), ("dt_f", heads_ssd),
             ("dt_b", heads_ssd), ("gate", N_BRANCH * d)]
    off, o = {}, 0
    for name, s in sizes:
        off[name] = (o, s)
        o += s
    col = lambda name, a=0, b=None: w_in[:, off[name][0] + a: off[name][0] + (off[name][1] if b is None else b)]
    small = jnp.zeros((d, LANES), F32)
    small = small.at[:, SM_AF:SM_AF + GLA_RANK].set(col("a_f"))
    small = small.at[:, SM_AB:SM_AB + GLA_RANK].set(col("a_b"))
    small = small.at[:, SM_DTF:SM_DTF + heads_ssd].set(col("dt_f"))
    small = small.at[:, SM_DTB:SM_DTB + heads_ssd].set(col("dt_b"))
    parts = [col("glu"), col("q"), col("k"), col("v"), col("r"), col("z"),
             col("xbc", 0, inner), col("xbc", inner), col("gate"), small]
    w_r = jnp.concatenate(parts, axis=1)
    n_real = w_r.shape[1]
    tn = dims["tn_in"]
    n_pad = -(-n_real // tn) * tn
    w_r = jnp.pad(w_r, ((0, 0), (0, n_pad - n_real))).astype(BF16)

    def wa_pad(wa, lane0):
        return jnp.zeros((LANES, qk), F32).at[lane0:lane0 + GLA_RANK].set(wa).astype(BF16)

    def lane_row(vec, lane0):
        return jnp.zeros((1, LANES), F32).at[0, lane0:lane0 + heads_ssd].set(vec)

    conv_w = p["ssd_conv_w"][l]
    conv_b = p["ssd_conv_b"][l][None, :]
    wr = p["w_router"][l]
    whi = wr.astype(BF16)
    r1 = wr - whi.astype(F32)
    wmid = r1.astype(BF16)
    wlo = (r1 - wmid.astype(F32)).astype(BF16)
    return dict(
        w_in=w_r,
        conv_w=p["conv_w"][l], conv_b=p["conv_b"][l][None], ln_g=p["conv_ln_g"][l][None],
        ln_b=p["conv_ln_b"][l][None],
        wa_f=wa_pad(p["gla_wa_f"][l], SM_AF), wa_b=wa_pad(p["gla_wa_b"][l], SM_AB),
        ba_f=p["gla_ba_f"][l][None], ba_b=p["gla_ba_b"][l][None],
        gla_g=p["gla_norm_g"][l][None],
        cw_x=conv_w[:, :inner], cw_bc=conv_w[:, inner:], cb_x=conv_b[:, :inner], cb_bc=conv_b[:, inner:],
        dtb=lane_row(p["ssd_dt_bias_f"][l], SM_DTF) + lane_row(p["ssd_dt_bias_b"][l], SM_DTB),
        a_f=lane_row(-jnp.exp(p["ssd_a_log_f"][l]), SM_DTF),
        a_b=lane_row(-jnp.exp(p["ssd_a_log_b"][l]), SM_DTB),
        dskip=jnp.repeat(p["ssd_d"][l], SSD_HEADDIM)[None],
        ssd_g=p["ssd_norm_g"][l][None],
        w_c=p["w_proj_conv"][l].astype(BF16), w_g=p["w_proj_gla"][l].astype(BF16),
        w_s=p["w_proj_ssd"][l].astype(BF16), w_out=p["w_out"][l].astype(BF16),
        wr3=jnp.stack([whi, wmid, wlo]), br=p["b_router"][l][None],
        we_in=p["w_exp_in"][l].astype(BF16), be_in=p["b_exp_in"][l][:, None, :],
        we_out=p["w_exp_out"][l].astype(BF16), be_out=p["b_exp_out"][l][:, None, :],
        g_mix=p["g_mix"][l][None], g_ffn=p["g_ffn"][l][None],
    )


def _expand_matrix(lane0, heads, inner):
    lanes = jnp.arange(LANES)[:, None]
    cols = jnp.arange(inner)[None, :]
    return (lanes == lane0 + cols // SSD_HEADDIM).astype(BF16)


def _stream(x, mod, lp, dims, cols, nseq, seq_len, seg_len, states, full, g_final=None):
    inner, qk = dims["inner"], dims["qk"]
    h = _norm_call(x, lp["g_mix"], mod[1], mod[0], seq_len, BF16)
    proj = _mm_call(h, lp["w_in"], 512, dims["tn_in"])

    blk = _pick(seq_len, 256)
    tri_lo = _tri_blockdiag(blk, CHUNK, False)
    tri_up = _tri_blockdiag(blk, CHUNK, True)
    heads = qk // GLA_DK
    o_f, gs_f = _gla_call(False, proj, cols, lp["wa_f"], lp["ba_f"], tri_lo, states[0], nseq, seq_len, heads)
    y_gla, gs_b = _gla_call(True, proj, cols, lp["wa_b"], lp["ba_b"], tri_up, states[1], nseq, seq_len,
                            heads, extra=(o_f, lp["gla_g"]))

    xs = _ssd_prep_call(proj, cols["xs"] * LANES // inner, inner, lp["cw_x"], lp["cb_x"], nseq, seq_len)
    wbc = 2 * SSD_GROUPS * SSD_STATE
    bc = _ssd_prep_call(proj, cols["bc"] * LANES // wbc, wbc, lp["cw_bc"], lp["cb_bc"], nseq, seq_len)
    full_lo = _tri_blockdiag(blk, blk, False)
    full_up = _tri_blockdiag(blk, blk, True)
    ssd_heads = inner // SSD_HEADDIM
    ex_f = _expand_matrix(SM_DTF, ssd_heads, inner)
    ex_b = _expand_matrix(SM_DTB, ssd_heads, inner)
    scols = dict(small=cols["small"], z=cols["z"] * LANES // inner)
    y_f, hs_f = _ssd_call(False, xs, bc, proj, scols, lp["dtb"], lp["a_f"], full_lo, ex_f, states[2],
                          nseq, seq_len)
    y_ssd, hs_b = _ssd_call(True, xs, bc, proj, scols, lp["dtb"], lp["a_b"], full_up, ex_b, states[3],
                            nseq, seq_len, extra=(y_f, lp["dskip"], lp["ssd_g"]))
    finals = (gs_f, gs_b, hs_f, hs_b)
    if not full:
        return None, finals

    y_conv = _conv_call(proj, lp["conv_w"], lp["conv_b"], lp["ln_g"], lp["ln_b"], seg_len)
    tn = dims["tn_merge"]
    merged = _merge_call(proj, cols["gate"] * LANES // tn, y_conv, y_gla, y_ssd,
                         lp["w_c"], lp["w_g"], lp["w_s"], tn)
    x1 = _out_call(merged, lp["w_out"], x, mod[2], seq_len)
    h2, top_i, top_w = _router_call(x1, lp["g_ffn"], mod[4], mod[3], lp["wr3"], lp["br"], seq_len)
    ne = lp["we_in"].shape[0]
    per_expert = max(64, x.shape[0] * TOP_K // ne)
    tm = min(MOE_TILE, 1 << (per_expert.bit_length() - 1))
    plan = _dispatch_plan(top_i[:, :TOP_K], ne, tm)
    y = _experts_call(h2, plan, lp["we_in"], lp["be_in"], lp["we_out"], lp["be_out"], tm)
    x2 = _combine_call(y, top_w, x1, mod[5], seq_len, g_final)
    return x2, finals


def kernel(x, c, ctx, c_ctx, g_mix, g_ffn, w_mod, b_mod, w_in, conv_w, conv_b, conv_ln_g, conv_ln_b,
           w_proj_conv, gla_wa_f, gla_ba_f, gla_wa_b, gla_ba_b, gla_norm_g, w_proj_gla, ssd_conv_w,
           ssd_conv_b, ssd_dt_bias_f, ssd_dt_bias_b, ssd_a_log_f, ssd_a_log_b, ssd_d, ssd_norm_g,
           w_proj_ssd, w_out, w_router, b_router, w_exp_in, b_exp_in, w_exp_out, b_exp_out, g_final):
    p = dict(g_mix=g_mix, g_ffn=g_ffn, w_in=w_in, conv_w=conv_w, conv_b=conv_b, conv_ln_g=conv_ln_g,
             conv_ln_b=conv_ln_b, w_proj_conv=w_proj_conv, gla_wa_f=gla_wa_f, gla_ba_f=gla_ba_f,
             gla_wa_b=gla_wa_b, gla_ba_b=gla_ba_b, gla_norm_g=gla_norm_g, w_proj_gla=w_proj_gla,
             ssd_conv_w=ssd_conv_w, ssd_conv_b=ssd_conv_b, ssd_dt_bias_f=ssd_dt_bias_f,
             ssd_dt_bias_b=ssd_dt_bias_b, ssd_a_log_f=ssd_a_log_f, ssd_a_log_b=ssd_a_log_b,
             ssd_d=ssd_d, ssd_norm_g=ssd_norm_g, w_proj_ssd=w_proj_ssd, w_out=w_out,
             w_router=w_router, b_router=b_router, w_exp_in=w_exp_in, b_exp_in=b_exp_in,
             w_exp_out=w_exp_out, b_exp_out=b_exp_out)
    bsz, t, d = x.shape
    ctx_len = ctx.shape[1]
    depth = w_mod.shape[0]
    ch = conv_w.shape[2]
    qk = gla_wa_f.shape[2]
    ssd_heads = ssd_d.shape[1]
    inner = ssd_heads * SSD_HEADDIM
    assert ssd_heads <= 16 and bsz + 1 <= 8
    widths = [("glu", 2 * ch), ("q", qk), ("k", qk), ("v", qk), ("r", qk), ("z", inner), ("xs", inner),
              ("bc", 2 * SSD_GROUPS * SSD_STATE), ("gate", N_BRANCH * d), ("small", LANES)]
    cols, o = {}, 0
    for name, wdt in widths:
        assert o % LANES == 0
        cols[name] = o // LANES
        o += wdt
    tn_merge = math.gcd(cols["gate"] * LANES, _pick(d, 1024, LANES))
    dims = dict(d=d, ch=ch, qk=qk, inner=inner, ssd_heads=ssd_heads,
                tn_in=min(1536, o), tn_merge=tn_merge)
    assert (cols["z"] * LANES) % inner == 0 and (cols["xs"] * LANES) % inner == 0
    assert (cols["bc"] * LANES) % (2 * SSD_GROUPS * SSD_STATE) == 0
    assert (cols["gate"] * LANES) % tn_merge == 0 and (cols["glu"] == 0)

    cvecs = jnp.concatenate([c, c_ctx[None, :]], axis=0)
    mods = _mod_call(cvecs, w_mod, b_mod[:, None, :])

    xl = x.reshape(bsz * t, d)
    xc = ctx.reshape(bsz * ctx_len, d)
    heads = qk // GLA_DK
    zero_g = jnp.zeros((bsz, heads, GLA_DV, GLA_DK), F32)
    zero_s = jnp.zeros((bsz, SSD_GROUPS, inner // SSD_GROUPS, SSD_STATE), F32)
    zeros = (zero_g, zero_g, zero_s, zero_s)
    for l in range(depth):
        lp = _layer_params(l, p, dims)
        m = mods[l].reshape(8, N_MOD, d)
        mod_l = jnp.transpose(m[:bsz], (1, 0, 2))[:, :, None, :]
        mod_c = m[bsz][:, None, None, :]
        last = l == depth - 1
        xc_new, ctx_states = _stream(xc, mod_c, lp, dims, cols, bsz, ctx_len, ctx_len, zeros,
                                     full=not last)
        xl, _ = _stream(xl, mod_l, lp, dims, cols, bsz, t, GRID_W, ctx_states, full=True,
                        g_final=g_final[None] if last else None)
        if not last:
            xc = xc_new
    return xl.reshape(bsz, t, d)
```
